```python
import jax, jax.numpy as jnp
from jax import lax
import numpy as np

D_MODEL = 1024
BATCH = 4
SEQ = 4096
DEPTH = 2

CTX_LEN = 256
GRID_W = 64
N_Q_HEADS = 8
N_KV_HEADS = 2
HEAD_DIM = 64
Q_PER_KV = N_Q_HEADS // N_KV_HEADS
ATTN_WIDTH = N_Q_HEADS * HEAD_DIM
KV_WIDTH = N_KV_HEADS * HEAD_DIM
WINDOW = 128
BLOCK = 128
ROPE_BASE = 10000.0
POOL_GROUPS = 4
POOL_GROUP_DIM = 64
POOL_WIDTH = POOL_GROUPS * POOL_GROUP_DIM
POOL_WINDOWS = (2, 4, 8, 16)
FOURIER_GROUPS = 4
FOURIER_GROUP_DIM = 64
FOURIER_WIDTH = FOURIER_GROUPS * FOURIER_GROUP_DIM
N_BRANCHES = 3
IN_SPLITS = (KV_WIDTH, 2 * KV_WIDTH, 2 * KV_WIDTH + ATTN_WIDTH,
             2 * KV_WIDTH + ATTN_WIDTH + POOL_WIDTH,
             2 * KV_WIDTH + ATTN_WIDTH + POOL_WIDTH + FOURIER_WIDTH)
IN_WIDTH = 2 * KV_WIDTH + ATTN_WIDTH + POOL_WIDTH + FOURIER_WIDTH + N_BRANCHES * D_MODEL
D_FF = 2816
CONV_WIDTH = 3
N_MOD = 6
EPS = 1e-6
NEG_INF = -1e30

kernel_name = "hybrid_gated_parallel_dit_block"


def rms_norm(x, g):
    xf = x.astype(jnp.float32)
    y = xf * lax.rsqrt(jnp.mean(xf * xf, axis=-1, keepdims=True) + EPS)
    return (y * g.astype(jnp.float32)).astype(x.dtype)


def adaln_params(cond, w_mod, b_mod):
    m = jax.nn.silu(cond) @ w_mod + b_mod
    return jnp.split(m[..., None, :], N_MOD, axis=-1)


def modulate(h, shift, scale):
    return h * (1 + scale) + shift


def axial_rope_tables(n_tokens):
    rows = n_tokens // GRID_W
    row = jnp.repeat(jnp.arange(rows, dtype=jnp.int32), GRID_W).astype(jnp.float32)
    col = jnp.tile(jnp.arange(GRID_W, dtype=jnp.int32), rows).astype(jnp.float32)
    n_freq = HEAD_DIM // 4
    inv_freq = ROPE_BASE ** (-jnp.arange(n_freq, dtype=jnp.float32) / n_freq)
    ang = jnp.stack([row[:, None] * inv_freq[None, :], col[:, None] * inv_freq[None, :]], axis=1)
    return jnp.cos(ang), jnp.sin(ang)


def apply_rope(x, cos, sin):
    B, T, H, _ = x.shape
    xr = x.astype(jnp.float32).reshape(B, T, H, 2, 2, HEAD_DIM // 4)
    c = cos[None, :, None, :, :]
    s = sin[None, :, None, :, :]
    x1, x2 = xr[..., 0, :], xr[..., 1, :]
    out = jnp.stack([x1 * c - x2 * s, x2 * c + x1 * s], axis=-2)
    return out.reshape(x.shape).astype(x.dtype)


def sink_column(sink, lead_shape):
    s = sink.astype(jnp.float32).reshape((1, N_KV_HEADS, Q_PER_KV) + (1,) * (len(lead_shape) - 2))
    return jnp.broadcast_to(s, tuple(lead_shape) + (1,))


def latent_attention(q, k, v, kc, vc, sink):
    B, T = q.shape[:2]
    nb = T // BLOCK
    ctx_len = kc.shape[1]
    scale = HEAD_DIM ** -0.5
    qb = q.reshape(B, nb, BLOCK, N_KV_HEADS, Q_PER_KV, HEAD_DIM)

    def band(a):
        a = a.reshape(B, nb, BLOCK, N_KV_HEADS, HEAD_DIM)
        pad = jnp.zeros_like(a[:, :1])
        ap = jnp.concatenate([pad, a, pad], axis=1)
        return jnp.concatenate([ap[:, :-2], ap[:, 1:-1], ap[:, 2:]], axis=2)

    kb, vb = band(k), band(v)
    s_lat = jnp.einsum('bnqhgd,bnkhd->bhgnqk', qb, kb).astype(jnp.float32) * scale
    blk = jnp.arange(nb, dtype=jnp.int32)[:, None]
    qpos = blk * BLOCK + jnp.arange(BLOCK, dtype=jnp.int32)[None, :]
    kpos = (blk - 1) * BLOCK + jnp.arange(3 * BLOCK, dtype=jnp.int32)[None, :]
    rel = kpos[:, None, :] - qpos[:, :, None]
    valid = (jnp.abs(rel) <= WINDOW) & (kpos[:, None, :] >= 0) & (kpos[:, None, :] < T)
    s_lat = jnp.where(valid, s_lat, NEG_INF)
    s_ctx = jnp.einsum('bnqhgd,bchd->bhgnqc', qb, kc).astype(jnp.float32) * scale
    s_all = jnp.concatenate([s_lat, s_ctx, sink_column(sink, s_lat.shape[:-1])], axis=-1)
    p = jax.nn.softmax(s_all, axis=-1)
    p_lat = p[..., :3 * BLOCK].astype(v.dtype)
    p_ctx = p[..., 3 * BLOCK:3 * BLOCK + ctx_len].astype(v.dtype)
    o = (jnp.einsum('bhgnqk,bnkhd->bnqhgd', p_lat, vb)
         + jnp.einsum('bhgnqc,bchd->bnqhgd', p_ctx, vc))
    return o.reshape(B, T, ATTN_WIDTH)


def context_attention(qc, kc, vc, sink):
    B, L = qc.shape[:2]
    scale = HEAD_DIM ** -0.5
    qg = qc.reshape(B, L, N_KV_HEADS, Q_PER_KV, HEAD_DIM)
    s = jnp.einsum('bqhgd,bkhd->bhgqk', qg, kc).astype(jnp.float32) * scale
    s_all = jnp.concatenate([s, sink_column(sink, s.shape[:-1])], axis=-1)
    p = jax.nn.softmax(s_all, axis=-1)[..., :L].astype(vc.dtype)
    o = jnp.einsum('bhgqk,bkhd->bqhgd', p, vc)
    return o.reshape(B, L, ATTN_WIDTH)


def multiscale_pool(u, pool_w, pool_scale):
    B, T = u.shape[:2]
    ug = u.reshape(B, T, POOL_GROUPS, POOL_GROUP_DIM).astype(jnp.float32)
    cs = jnp.concatenate([jnp.zeros_like(ug[:, :1]), jnp.cumsum(ug, axis=1)], axis=1)
    w = jnp.array(POOL_WINDOWS, dtype=jnp.int32)[None, :]
    t = jnp.arange(T, dtype=jnp.int32)[:, None]
    lo = jnp.clip(t - w // 2, 0, T)
    hi = jnp.clip(t - w // 2 + w, 0, T)
    gidx = jnp.arange(POOL_GROUPS, dtype=jnp.int32)[None, :]
    win_sum = cs[:, hi, gidx] - cs[:, lo, gidx]
    cnt = (hi - lo).astype(jnp.float32)[None, :, :, None]
    pooled = (win_sum / cnt - ug).astype(u.dtype)
    y = jnp.einsum('btgc,gcd->btgd', pooled, pool_w).reshape(B, T, POOL_WIDTH)
    return y * pool_scale


def fourier_mix(f):
    B, T = f.shape[:2]
    fg = f.reshape(B, T, FOURIER_GROUPS, FOURIER_GROUP_DIM).astype(jnp.float32)
    y = jnp.fft.fft2(fg, axes=(1, 3), norm='ortho').real
    return y.astype(f.dtype).reshape(B, T, FOURIER_WIDTH)


def merge_branches(o_attn, pool_in, four_in, gate_in, pool_w, pool_scale,
                   w_br_attn, w_br_pool, w_br_four, w_out):
    g_a, g_p, g_f = jnp.split(jax.nn.sigmoid(gate_in), N_BRANCHES, axis=-1)
    y = (g_a * (o_attn @ w_br_attn)
         + g_p * (multiscale_pool(pool_in, pool_w, pool_scale) @ w_br_pool)
         + g_f * (fourier_mix(four_in) @ w_br_four))
    return y @ w_out


def dwconv_centred(u, w):
    T = u.shape[1]
    up = jnp.pad(u, ((0, 0), (1, 1), (0, 0)))
    return up[:, :T] * w[0] + up[:, 1:T + 1] * w[1] + up[:, 2:] * w[2]


def conv_ffn(h, w_up, conv_w, w_down):
    a = dwconv_centred(h @ w_up, conv_w)
    val, gate = jnp.split(a, 2, axis=-1)
    return (val * jax.nn.silu(gate)) @ w_down


def setup_inputs(seed: int = 0) -> dict:
    key = jax.random.key(seed)
    ks = jax.random.split(key, 24)
    f32 = jnp.float32
    n = lambda k, shape, s: jax.random.normal(k, shape, f32) * s
    return {
        'x': n(ks[0], (BATCH, SEQ, D_MODEL), 1.0),
        'c': n(ks[1], (BATCH, D_MODEL), 1.0),
        'ctx': n(ks[2], (BATCH, CTX_LEN, D_MODEL), 1.0),
        'c_ctx': n(ks[3], (D_MODEL,), 1.0),
        'w_mod': n(ks[4], (DEPTH, D_MODEL, N_MOD * D_MODEL), 0.5 * D_MODEL ** -0.5),
        'b_mod': n(ks[5], (DEPTH, N_MOD * D_MODEL), 0.02),
        'norm_mix': 1.0 + n(ks[6], (DEPTH, D_MODEL), 0.02),
        'norm_ffn': 1.0 + n(ks[7], (DEPTH, D_MODEL), 0.02),
        'w_in': n(ks[8], (DEPTH, D_MODEL, IN_WIDTH), D_MODEL ** -0.5),
        'attn_sink': n(ks[9], (DEPTH, N_Q_HEADS), 0.5),
        'pool_w': n(ks[10], (DEPTH, POOL_GROUPS, POOL_GROUP_DIM, POOL_GROUP_DIM), POOL_GROUP_DIM ** -0.5),
        'pool_scale': 1.0 + n(ks[11], (DEPTH, POOL_WIDTH), 0.02),
        'w_br_attn': n(ks[12], (DEPTH, ATTN_WIDTH, D_MODEL), ATTN_WIDTH ** -0.5),
        'w_br_pool': n(ks[13], (DEPTH, POOL_WIDTH, D_MODEL), POOL_WIDTH ** -0.5),
        'w_br_four': n(ks[14], (DEPTH, FOURIER_WIDTH, D_MODEL), FOURIER_WIDTH ** -0.5),
        'w_out': n(ks[15], (DEPTH, D_MODEL, D_MODEL), D_MODEL ** -0.5),
        'w_up': n(ks[16], (DEPTH, D_MODEL, 2 * D_FF), D_MODEL ** -0.5),
        'conv_w': n(ks[17], (DEPTH, CONV_WIDTH, 2 * D_FF), CONV_WIDTH ** -0.5),
        'w_down': n(ks[18], (DEPTH, D_FF, D_MODEL), D_FF ** -0.5),
        'norm_final': 1.0 + n(ks[19], (D_MODEL,), 0.02),
    }


def reference(x, c, ctx, c_ctx, w_mod, b_mod, norm_mix, norm_ffn, w_in, attn_sink, pool_w, pool_scale,
              w_br_attn, w_br_pool, w_br_four, w_out, w_up, conv_w, w_down, norm_final):
    B, T, _ = x.shape
    cos, sin = axial_rope_tables(T)
    xc = ctx
    for l in range(DEPTH):
        last = l == DEPTH - 1
        sh1, sc1, g1, sh2, sc2, g2 = adaln_params(c, w_mod[l], b_mod[l])
        csh1, csc1, cg1, csh2, csc2, cg2 = adaln_params(c_ctx, w_mod[l], b_mod[l])

        h = modulate(rms_norm(x, norm_mix[l]), sh1, sc1)
        hc = modulate(rms_norm(xc, norm_mix[l]), csh1, csc1)
        k, v, q, u, f, gt = jnp.split(h @ w_in[l], IN_SPLITS, axis=-1)
        if last:
            kc, vc = jnp.split(hc @ w_in[l][:, :2 * KV_WIDTH], 2, axis=-1)
        else:
            kc, vc, qc, uc, fc, gtc = jnp.split(hc @ w_in[l], IN_SPLITS, axis=-1)
        L = hc.shape[1]
        q = apply_rope(q.reshape(B, T, N_Q_HEADS, HEAD_DIM), cos, sin)
        k = apply_rope(k.reshape(B, T, N_KV_HEADS, HEAD_DIM), cos, sin)
        v = v.reshape(B, T, N_KV_HEADS, HEAD_DIM)
        kc = kc.reshape(B, L, N_KV_HEADS, HEAD_DIM)
        vc = vc.reshape(B, L, N_KV_HEADS, HEAD_DIM)
        o = latent_attention(q, k, v, kc, vc, attn_sink[l])
        x = x + g1 * merge_branches(o, u, f, gt, pool_w[l], pool_scale[l],
                                    w_br_attn[l], w_br_pool[l], w_br_four[l], w_out[l])
        if not last:
            oc = context_attention(qc.reshape(B, L, N_Q_HEADS, HEAD_DIM), kc, vc, attn_sink[l])
            xc = xc + cg1 * merge_branches(oc, uc, fc, gtc, pool_w[l], pool_scale[l],
                                           w_br_attn[l], w_br_pool[l], w_br_four[l], w_out[l])

        hf = modulate(rms_norm(x, norm_ffn[l]), sh2, sc2)
        x = x + g2 * conv_ffn(hf, w_up[l], conv_w[l], w_down[l])
        if not last:
            hcf = modulate(rms_norm(xc, norm_ffn[l]), csh2, csc2)
            xc = xc + cg2 * conv_ffn(hcf, w_up[l], conv_w[l], w_down[l])
    return rms_norm(x, norm_final)
```

```python
import functools
import math

import numpy as np
import jax
import jax.numpy as jnp
from jax import lax
from jax.experimental import pallas as pl
from jax.experimental.pallas import tpu as pltpu

D_MODEL = 1024
N_Q_HEADS = 8
N_KV_HEADS = 2
HEAD_DIM = 64
Q_PER_KV = N_Q_HEADS // N_KV_HEADS
ATTN_WIDTH = N_Q_HEADS * HEAD_DIM
KV_WIDTH = N_KV_HEADS * HEAD_DIM
BLOCK = 128
GRID_W = 64
ROPE_BASE = 10000.0
POOL_WIDTH = 256
POOL_GROUP_DIM = 64
FOURIER_WIDTH = 256
FOURIER_GROUP_DIM = 64
D_FF = 2816
N_MOD = 6
EPS = 1e-6
NEG_INF = -1e30

LANES = 128
SUBLANES = 8
VMEM_LIMIT_BYTES = 56 * 1024 * 1024

TOKEN_TILE = 512
FF_CHUNK = 256
MERGE_CHUNK = 256
POOL_HALO = 8
ADALN_TILE = 1536
COND_ROWS = 8

F32 = jnp.float32
BF16 = jnp.bfloat16


def _dot(a, b):
    return jnp.dot(a, b, preferred_element_type=F32)


def _dot_f32(a, b):
    return jnp.dot(a, b, preferred_element_type=F32, precision=lax.Precision.HIGHEST)


def _dot_nt(a, b):
    return lax.dot_general(a, b, (((1,), (1,)), ((), ())), preferred_element_type=F32)


def _modnorm(x, g, sc, sh):
    ms = jnp.mean(x * x, axis=-1, keepdims=True)
    y = x * lax.rsqrt(ms + EPS) * g
    return y * (1.0 + sc) + sh


def _resident(shape):
    nd = len(shape)
    return pl.BlockSpec(shape, lambda *_: (0,) * nd, pipeline_mode=pl.Buffered(1))


def _mod_spec(j, ctx_row):
    if ctx_row is None:
        return pl.BlockSpec((None, None, 1, D_MODEL), lambda b, i: (b, j, 0, 0))
    return pl.BlockSpec((None, None, 1, D_MODEL), lambda b, i: (ctx_row, j, 0, 0))


def _params(*sem):
    return pltpu.CompilerParams(dimension_semantics=sem, vmem_limit_bytes=VMEM_LIMIT_BYTES)


def _adaln_kernel(c_ref, w_ref, b_ref, o_ref):
    c = c_ref[...]
    a = (c * jax.nn.sigmoid(c)).astype(BF16)
    o_ref[...] = _dot(a, w_ref[...].astype(BF16)) + b_ref[...]


def _adaln(cond, w_mod, b_mod):
    depth, _, width = w_mod.shape
    return pl.pallas_call(
        _adaln_kernel,
        grid=(depth, width // ADALN_TILE),
        in_specs=[
            pl.BlockSpec((COND_ROWS, D_MODEL), lambda l, j: (0, 0)),
            pl.BlockSpec((None, D_MODEL, ADALN_TILE), lambda l, j: (l, 0, j)),
            pl.BlockSpec((None, 1, ADALN_TILE), lambda l, j: (l, 0, j)),
        ],
        out_specs=pl.BlockSpec((None, COND_ROWS, ADALN_TILE), lambda l, j: (l, 0, j)),
        out_shape=jax.ShapeDtypeStruct((depth, COND_ROWS, width), F32),
        compiler_params=_params("arbitrary", "arbitrary"),
        name="adaln",
    )(cond, w_mod, b_mod.reshape(depth, 1, width))


_K_COLS = 2 * KV_WIDTH
_A_SPLITS = (0, _K_COLS, _K_COLS + KV_WIDTH, _K_COLS + KV_WIDTH + ATTN_WIDTH,
             _K_COLS + KV_WIDTH + ATTN_WIDTH + POOL_WIDTH,
             _K_COLS + KV_WIDTH + ATTN_WIDTH + POOL_WIDTH + FOURIER_WIDTH)


def _inproj_kernel(x_ref, g_ref, sh_ref, sc_ref, w_ref, cos_ref, sin_ref,
                   k_ref, v_ref, q_ref, u_ref, f_ref):
    h = _modnorm(x_ref[...], g_ref[...], sc_ref[...], sh_ref[...]).astype(BF16)
    cos = cos_ref[...]
    sin = sin_ref[...]
    lane = lax.broadcasted_iota(jnp.int32, cos.shape, 1)
    first = (lane & 31) < 16

    def rope(t):
        partner = jnp.where(first, pltpu.roll(t, LANES - 16, 1), pltpu.roll(t, 16, 1))
        return t * cos + partner * sin

    k0, v0, q0, u0, f0, end = _A_SPLITS
    for j in range(_K_COLS // LANES):
        c0 = k0 + j * LANES
        k_ref[:, j * LANES:(j + 1) * LANES] = rope(_dot(h, w_ref[:, c0:c0 + LANES])).astype(BF16)
    v_ref[...] = _dot(h, w_ref[:, v0:q0]).astype(BF16)
    scale = HEAD_DIM ** -0.5
    for j in range(ATTN_WIDTH // LANES):
        c0 = q0 + j * LANES
        q_ref[:, j * LANES:(j + 1) * LANES] = (rope(_dot(h, w_ref[:, c0:c0 + LANES])) * scale).astype(BF16)
    u_ref[...] = _dot(h, w_ref[:, u0:f0])
    f_ref[...] = _dot(h, w_ref[:, f0:end])


def _inproj(x, mod, gain, w_a, cos_t, sin_t, ctx_row):
    B, T, _ = x.shape
    tm = min(TOKEN_TILE, T)
    tok = lambda w: pl.BlockSpec((None, tm, w), lambda b, i: (b, i, 0))
    return pl.pallas_call(
        _inproj_kernel,
        grid=(B, T // tm),
        in_specs=[
            tok(D_MODEL),
            _resident((1, D_MODEL)),
            _mod_spec(0, ctx_row), _mod_spec(1, ctx_row),
            _resident(w_a.shape),
            pl.BlockSpec((tm, LANES), lambda b, i: (i, 0)),
            pl.BlockSpec((tm, LANES), lambda b, i: (i, 0)),
        ],
        out_specs=[tok(_K_COLS), tok(KV_WIDTH), tok(ATTN_WIDTH), tok(POOL_WIDTH), tok(FOURIER_WIDTH)],
        out_shape=[
            jax.ShapeDtypeStruct((B, T, _K_COLS), BF16),
            jax.ShapeDtypeStruct((B, T, KV_WIDTH), BF16),
            jax.ShapeDtypeStruct((B, T, ATTN_WIDTH), BF16),
            jax.ShapeDtypeStruct((B, T, POOL_WIDTH), F32),
            jax.ShapeDtypeStruct((B, T, FOURIER_WIDTH), F32),
        ],
        compiler_params=_params("arbitrary", "arbitrary"),
        name="inproj",
    )(x, gain, mod, mod, w_a, cos_t, sin_t)


def _attn_kernel(sink_ref, q_ref, *refs, nb, band):
    if band:
        kp_ref, km_ref, kn_ref, vp_ref, vm_ref, vn_ref, kc_ref, vc_ref, o_ref = refs
    else:
        kc_ref, vc_ref, o_ref = refs
    n = pl.program_id(1)
    rows = Q_PER_KV * BLOCK
    lane = lax.broadcasted_iota(jnp.int32, (BLOCK, LANES), 1)
    low = lane < HEAD_DIM
    rgrp = lax.broadcasted_iota(jnp.int32, (rows, 1), 0) // BLOCK
    if band:
        qi = lax.broadcasted_iota(jnp.int32, (rows, BLOCK), 0) % BLOCK
        kj = lax.broadcasted_iota(jnp.int32, (rows, BLOCK), 1)
        ok_prev = jnp.logical_and(kj >= qi, n > 0)
        ok_next = jnp.logical_and(kj <= qi, n < nb - 1)
    outs = []
    for h in range(N_KV_HEADS):
        qs = []
        for t in range(Q_PER_KV // 2):
            c0 = (h * (Q_PER_KV // 2) + t) * LANES
            qt = q_ref[:, c0:c0 + LANES]
            qs.append(jnp.where(low, qt, jnp.zeros_like(qt)))
            qs.append(jnp.where(low, jnp.zeros_like(qt), qt))
        qh = jnp.concatenate(qs, axis=0)
        ks = slice(h * LANES, (h + 1) * LANES)
        scores, values = [], []
        if band:
            scores.append(jnp.where(ok_prev, _dot_nt(qh, kp_ref[:, ks]), NEG_INF))
            scores.append(_dot_nt(qh, km_ref[:, ks]))
            scores.append(jnp.where(ok_next, _dot_nt(qh, kn_ref[:, ks]), NEG_INF))
            values += [vp_ref[...], vm_ref[...], vn_ref[...]]
        scores.append(_dot_nt(qh, kc_ref[:, ks]))
        values.append(vc_ref[...])
        sink = jnp.zeros((rows, 1), F32)
        for g in range(Q_PER_KV):
            sink = jnp.where(rgrp == g, sink_ref[h * Q_PER_KV + g], sink)
        m = sink
        for s in scores:
            m = jnp.maximum(m, jnp.max(s, axis=-1, keepdims=True))
        denom = jnp.exp(sink - m)
        o = jnp.zeros((rows, LANES), F32)
        for s, v in zip(scores, values):
            e = jnp.exp(s - m)
            denom = denom + jnp.sum(e, axis=-1, keepdims=True)
            o = o + _dot(e.astype(BF16), v)
        outs.append(o / denom)
    for g in range(Q_PER_KV):
        r = slice(g * BLOCK, (g + 1) * BLOCK)
        o_ref[:, g * LANES:(g + 1) * LANES] = jnp.where(low, outs[0][r], outs[1][r]).astype(BF16)


def _attention(q, k, v, kc, vc, sink, band):
    B, T, _ = q.shape
    L = kc.shape[1]
    nb = T // BLOCK
    qspec = pl.BlockSpec((None, BLOCK, ATTN_WIDTH), lambda b, n: (b, n, 0))
    kprev = lambda w: pl.BlockSpec((None, BLOCK, w), lambda b, n: (b, jnp.maximum(n - 1, 0), 0))
    kmid = lambda w: pl.BlockSpec((None, BLOCK, w), lambda b, n: (b, n, 0))
    knext = lambda w: pl.BlockSpec((None, BLOCK, w), lambda b, n: (b, jnp.minimum(n + 1, nb - 1), 0))
    cspec = lambda w: pl.BlockSpec((None, L, w), lambda b, n: (b, 0, 0))
    in_specs = [pl.BlockSpec(memory_space=pltpu.SMEM), qspec]
    args = [sink, q]
    if band:
        in_specs += [kprev(_K_COLS), kmid(_K_COLS), knext(_K_COLS),
                     kprev(KV_WIDTH), kmid(KV_WIDTH), knext(KV_WIDTH)]
        args += [k, k, k, v, v, v]
    in_specs += [cspec(_K_COLS), cspec(KV_WIDTH)]
    args += [kc, vc]
    return pl.pallas_call(
        functools.partial(_attn_kernel, nb=nb, band=band),
        grid=(B, nb),
        in_specs=in_specs,
        out_specs=pl.BlockSpec((None, BLOCK, ATTN_WIDTH), lambda b, n: (b, n, 0)),
        out_shape=jax.ShapeDtypeStruct((B, T, ATTN_WIDTH), BF16),
        compiler_params=_params("arbitrary", "arbitrary"),
        name="attention_band" if band else "attention_ctx",
    )(*args)


def _pool_kernel(u_ref, w_ref, s_ref, o_ref, pad_ref, *, T, tc):
    zeros = jnp.zeros((POOL_HALO, POOL_WIDTH), F32)
    pad_ref[0:POOL_HALO, :] = zeros
    pad_ref[POOL_HALO + T:POOL_HALO + T + POOL_HALO, :] = zeros
    pad_ref[POOL_HALO:POOL_HALO + T, :] = u_ref[...]
    lane = lax.broadcasted_iota(jnp.int32, (tc, LANES), 1)
    upper = lane >= POOL_GROUP_DIM
    row = lax.broadcasted_iota(jnp.int32, (tc, LANES), 0)
    for c in range(T // tc):
        base = POOL_HALO + c * tc
        t = row + c * tc
        pooled = []
        for tile in range(POOL_WIDTH // LANES):
            w_small = 2 << (2 * tile)
            w_big = 2 * w_small
            cols = slice(tile * LANES, (tile + 1) * LANES)
            load = lambda d: pad_ref[base + d:base + d + tc, cols]
            inner = load(-(w_small // 2))
            for d in range(-(w_small // 2) + 1, w_small // 2):
                inner = inner + load(d)
            outer = load(-(w_big // 2))
            for d in list(range(-(w_big // 2) + 1, -(w_small // 2))) + list(range(w_small // 2, w_big // 2)):
                outer = outer + load(d)
            win = inner + jnp.where(upper, outer, 0.0)
            half = jnp.where(upper, w_big // 2, w_small // 2)
            lo = jnp.maximum(t - half, 0)
            hi = jnp.minimum(t + half, T)
            cnt = (hi - lo).astype(F32)
            pooled.append(win / cnt - load(0))
        p = jnp.concatenate(pooled, axis=1).astype(BF16)
        y = _dot(p, w_ref[...]) * s_ref[...]
        o_ref[c * tc:(c + 1) * tc, :] = y.astype(BF16)


def _pool(u, w_bd, scale):
    B, T, _ = u.shape
    tc = min(TOKEN_TILE, T)
    return pl.pallas_call(
        functools.partial(_pool_kernel, T=T, tc=tc),
        grid=(B,),
        in_specs=[
            pl.BlockSpec((None, T, POOL_WIDTH), lambda b: (b, 0, 0)),
            _resident(w_bd.shape),
            _resident((1, POOL_WIDTH)),
        ],
        out_specs=pl.BlockSpec((None, T, POOL_WIDTH), lambda b: (b, 0, 0)),
        out_shape=jax.ShapeDtypeStruct((B, T, POOL_WIDTH), BF16),
        scratch_shapes=[pltpu.VMEM((T + 2 * POOL_HALO, POOL_WIDTH), F32)],
        compiler_params=_params("arbitrary"),
        name="pool",
    )(u, w_bd, scale)


def _fourier_factors(T):
    n1 = 1 << (int(math.log2(T)) // 2)
    return n1, T // n1


def _fourier_tables(T):
    n1, n2 = _fourier_factors(T)
    gd = FOURIER_GROUP_DIM
    c = np.arange(gd)
    ang = 2.0 * np.pi * np.outer(c, c) / gd
    eye = np.eye(FOURIER_WIDTH // gd)
    cd = np.concatenate([np.kron(eye, np.cos(ang)), -np.kron(eye, np.sin(ang))], axis=1) / math.sqrt(gd)
    t2 = np.arange(n2)[:, None, None]
    k1 = np.arange(n1)[None, :, None]
    t1 = np.arange(n1)[None, None, :]
    ph = -2.0 * np.pi * (t2 * k1 / T + t1 * k1 / n1)
    m1 = np.concatenate([np.cos(ph), np.sin(ph)], axis=1) / math.sqrt(n1)
    a3 = 2.0 * np.pi * np.outer(np.arange(n2), np.arange(n2)) / n2
    m3 = np.stack([np.cos(a3), np.sin(a3)]) / math.sqrt(n2)
    return (jnp.asarray(cd, F32), jnp.asarray(m1, F32), jnp.asarray(m3, F32))


def _fourier_kernel(f_ref, cd_ref, m1_ref, m3_ref, o_ref, z_ref, b_ref, y_ref, *, n1, n2, tc):
    T = n1 * n2
    W = FOURIER_WIDTH
    wt = W // LANES
    lanes = lambda j: slice(j * LANES, (j + 1) * LANES)

    for c in range(T // tc):
        r = slice(c * tc, (c + 1) * tc)
        z = _dot_f32(f_ref[r, :], cd_ref[...])
        for j in range(2 * wt):
            z_ref[j, r, :] = z[:, lanes(j)]

    def over_t1(t2, carry):
        zs = jnp.concatenate([z_ref[j, pl.ds(t2, n1, stride=n2), :] for j in range(2 * wt)], axis=1)
        p = _dot_f32(m1_ref[t2], zs)
        br = p[:n1, 0:W] - p[n1:, W:]
        bi = p[:n1, W:] + p[n1:, 0:W]
        off = pl.multiple_of(t2 * n1, n1)
        for j in range(wt):
            b_ref[j, pl.ds(off, n1), :] = br[:, lanes(j)]
            b_ref[wt + j, pl.ds(off, n1), :] = bi[:, lanes(j)]
        return carry

    lax.fori_loop(0, n2, over_t1, 0)

    def over_t2(k1, carry):
        rows = pl.ds(k1, n2, stride=n1)
        br = jnp.concatenate([b_ref[j, rows, :] for j in range(wt)], axis=1)
        bi = jnp.concatenate([b_ref[wt + j, rows, :] for j in range(wt)], axis=1)
        y = _dot_f32(m3_ref[0], br) + _dot_f32(m3_ref[1], bi)
        for j in range(wt):
            y_ref[j, rows, :] = y[:, lanes(j)]
        return carry

    lax.fori_loop(0, n1, over_t2, 0)
    for j in range(wt):
        o_ref[:, lanes(j)] = y_ref[j].astype(BF16)


def _fourier(f, tables):
    B, T, W = f.shape
    n1, n2 = _fourier_factors(T)
    cd, m1, m3 = tables
    return pl.pallas_call(
        functools.partial(_fourier_kernel, n1=n1, n2=n2, tc=min(TOKEN_TILE, T)),
        grid=(B,),
        in_specs=[
            pl.BlockSpec((None, T, W), lambda b: (b, 0, 0)),
            _resident(cd.shape), _resident(m1.shape), _resident(m3.shape),
        ],
        out_specs=pl.BlockSpec((None, T, W), lambda b: (b, 0, 0)),
        out_shape=jax.ShapeDtypeStruct((B, T, W), BF16),
        scratch_shapes=[pltpu.VMEM((2 * W // LANES, T, LANES), F32), pltpu.VMEM((2 * W // LANES, T, LANES), F32),
                        pltpu.VMEM((W // LANES, T, LANES), F32)],
        compiler_params=_params("arbitrary"),
        name="fourier",
    )(f, cd, m1, m3)


def _merge_kernel(x_ref, o_ref, p_ref, f_ref, g_ref, sh_ref, sc_ref, gt_ref,
                  wg_ref, wa_ref, wp_ref, wf_ref, wo_ref, out_ref, y_ref):
    x = x_ref[...]
    h = _modnorm(x, g_ref[...], sc_ref[...], sh_ref[...]).astype(BF16)
    o = o_ref[...]
    p = p_ref[...]
    f = f_ref[...]
    for j in range(D_MODEL // MERGE_CHUNK):
        c = slice(j * MERGE_CHUNK, (j + 1) * MERGE_CHUNK)
        gate = lambda b: jax.nn.sigmoid(_dot(h, wg_ref[:, b * D_MODEL + j * MERGE_CHUNK:
                                                        b * D_MODEL + (j + 1) * MERGE_CHUNK]))
        y = (gate(0) * _dot(o, wa_ref[:, c]) + gate(1) * _dot(p, wp_ref[:, c])
             + gate(2) * _dot(f, wf_ref[:, c]))
        y_ref[:, c] = y.astype(BF16)
    out_ref[...] = x + gt_ref[...] * _dot(y_ref[...], wo_ref[...])


def _merge(x, o, p, f, mod, gain, wg, wa, wp, wf, wo, ctx_row):
    B, T, _ = x.shape
    tm = min(TOKEN_TILE, T)
    tok = lambda w: pl.BlockSpec((None, tm, w), lambda b, i: (b, i, 0))
    return pl.pallas_call(
        _merge_kernel,
        grid=(B, T // tm),
        in_specs=[
            tok(D_MODEL), tok(ATTN_WIDTH), tok(POOL_WIDTH), tok(FOURIER_WIDTH),
            _resident((1, D_MODEL)),
            _mod_spec(0, ctx_row), _mod_spec(1, ctx_row), _mod_spec(2, ctx_row),
            _resident(wg.shape), _resident(wa.shape), _resident(wp.shape), _resident(wf.shape),
            _resident(wo.shape),
        ],
        out_specs=tok(D_MODEL),
        out_shape=jax.ShapeDtypeStruct((B, T, D_MODEL), F32),
        scratch_shapes=[pltpu.VMEM((tm, D_MODEL), BF16)],
        compiler_params=_params("arbitrary", "arbitrary"),
        name="merge",
    )(x, o, p, f, gain, mod, mod, mod, wg, wa, wp, wf, wo)


def _ffn_kernel(x_ref, xp_ref, xn_ref, g_ref, sh_ref, sc_ref, gt_ref, wu_ref, cw_ref, wd_ref, gf_ref,
                out_ref, h_ref, a_ref, *, tm, final):
    i = pl.program_id(1)
    last = pl.num_programs(1) - 1
    g, sh, sc = g_ref[...], sh_ref[...], sc_ref[...]
    x = x_ref[...]
    H = SUBLANES
    h_ref[H:H + tm, :] = _modnorm(x, g, sc, sh).astype(BF16)
    keep_prev = (i > 0).astype(F32)
    keep_next = (i < last).astype(F32)
    h_ref[0:H, :] = (_modnorm(xp_ref[...], g, sc, sh) * keep_prev).astype(BF16)
    h_ref[H + tm:H + tm + H, :] = (_modnorm(xn_ref[...], g, sc, sh) * keep_next).astype(BF16)
    hf = h_ref[...]
    rows = tm + 2 * H

    def conv(up, cols):
        w = cw_ref[:, cols]
        y = (pltpu.roll(up, 1, 0) * w[0:1] + up * w[1:2] + pltpu.roll(up, rows - 1, 0) * w[2:3])
        return y[H:H + tm]

    for j in range(D_FF // FF_CHUNK):
        cv = slice(j * FF_CHUNK, (j + 1) * FF_CHUNK)
        cg = slice(D_FF + j * FF_CHUNK, D_FF + (j + 1) * FF_CHUNK)
        val = conv(_dot(hf, wu_ref[:, cv]), cv)
        gate = conv(_dot(hf, wu_ref[:, cg]), cg)
        a_ref[:, cv] = (val * (gate * jax.nn.sigmoid(gate))).astype(BF16)
    y = x + gt_ref[...] * _dot(a_ref[...], wd_ref[...])
    if final:
        ms = jnp.mean(y * y, axis=-1, keepdims=True)
        y = y * lax.rsqrt(ms + EPS) * gf_ref[...]
    out_ref[...] = y


def _ffn(x, mod, gain, wu, cw, wd, gain_final, ctx_row, final):
    B, T, _ = x.shape
    tm = min(TOKEN_TILE, T)
    per = tm // SUBLANES
    nblk = T // SUBLANES
    tok = pl.BlockSpec((None, tm, D_MODEL), lambda b, i: (b, i, 0))
    prev = pl.BlockSpec((None, SUBLANES, D_MODEL), lambda b, i: (b, jnp.maximum(i * per - 1, 0), 0))
    nxt = pl.BlockSpec((None, SUBLANES, D_MODEL), lambda b, i: (b, jnp.minimum((i + 1) * per, nblk - 1), 0))
    return pl.pallas_call(
        functools.partial(_ffn_kernel, tm=tm, final=final),
        grid=(B, T // tm),
        in_specs=[
            tok, prev, nxt,
            _resident((1, D_MODEL)),
            _mod_spec(3, ctx_row), _mod_spec(4, ctx_row), _mod_spec(5, ctx_row),
            _resident(wu.shape), _resident(cw.shape), _resident(wd.shape),
            _resident((1, D_MODEL)),
        ],
        out_specs=tok,
        out_shape=jax.ShapeDtypeStruct((B, T, D_MODEL), F32),
        scratch_shapes=[pltpu.VMEM((tm + 2 * SUBLANES, D_MODEL), BF16), pltpu.VMEM((tm, D_FF), BF16)],
        compiler_params=_params("arbitrary", "arbitrary"),
        name="ffn_final" if final else "ffn",
    )(x, x, x, gain, mod, mod, mod, wu, cw, wd, gain_final)


def _rope_tables(T):
    rows = T // GRID_W
    row = jnp.repeat(jnp.arange(rows, dtype=jnp.int32), GRID_W).astype(F32)
    col = jnp.tile(jnp.arange(GRID_W, dtype=jnp.int32), rows).astype(F32)
    n_freq = HEAD_DIM // 4
    inv_freq = ROPE_BASE ** (-jnp.arange(n_freq, dtype=F32) / n_freq)
    ar = row[:, None] * inv_freq[None, :]
    ac = col[:, None] * inv_freq[None, :]
    cos = jnp.concatenate([jnp.cos(ar), jnp.cos(ar), jnp.cos(ac), jnp.cos(ac)], axis=1)
    sin = jnp.concatenate([-jnp.sin(ar), jnp.sin(ar), -jnp.sin(ac), jnp.sin(ac)], axis=1)
    reps = LANES // HEAD_DIM
    return jnp.tile(cos, (1, reps)), jnp.tile(sin, (1, reps))


def _block_diag(w):
    g, a, b = w.shape
    out = jnp.zeros((g * a, g * b), w.dtype)
    for i in range(g):
        out = out.at[i * a:(i + 1) * a, i * b:(i + 1) * b].set(w[i])
    return out


def kernel(x, c, ctx, c_ctx, w_mod, b_mod, norm_mix, norm_ffn, w_in, attn_sink, pool_w, pool_scale,
           w_br_attn, w_br_pool, w_br_four, w_out, w_up, conv_w, w_down, norm_final):
    B, T, _ = x.shape
    L = ctx.shape[1]
    depth = w_mod.shape[0]
    assert B < COND_ROWS and T % TOKEN_TILE == 0 and T % BLOCK == 0 and L % BLOCK == 0

    cond = jnp.zeros((COND_ROWS, D_MODEL), F32).at[:B].set(c).at[B].set(c_ctx)
    mods = _adaln(cond, w_mod, b_mod)

    cos_t, sin_t = _rope_tables(T)
    cos_c = jnp.ones((L, LANES), F32)
    sin_c = jnp.zeros((L, LANES), F32)
    four_lat = _fourier_tables(T)
    four_ctx = _fourier_tables(L)
    row2 = lambda v: v.reshape(1, -1)
    gain_final = row2(norm_final)

    xc = ctx
    for l in range(depth):
        last = l == depth - 1
        mod = mods[l].reshape(COND_ROWS, N_MOD, 1, D_MODEL)
        wl = w_in[l]
        kw = wl[:, :KV_WIDTH].reshape(D_MODEL, N_KV_HEADS, 1, HEAD_DIM)
        kdup = jnp.broadcast_to(kw, (D_MODEL, N_KV_HEADS, 2, HEAD_DIM)).reshape(D_MODEL, _K_COLS)
        n_tok = 2 * KV_WIDTH + ATTN_WIDTH + POOL_WIDTH + FOURIER_WIDTH
        w_a = jnp.concatenate([kdup, wl[:, KV_WIDTH:n_tok]], axis=1).astype(BF16)
        w_g = wl[:, n_tok:].astype(BF16)
        w_a_br = (w_br_attn[l].reshape(N_KV_HEADS, Q_PER_KV, HEAD_DIM, D_MODEL)
                  .transpose(1, 0, 2, 3).reshape(ATTN_WIDTH, D_MODEL).astype(BF16))
        w_p_br = w_br_pool[l].astype(BF16)
        w_f_br = w_br_four[l].astype(BF16)
        w_o = w_out[l].astype(BF16)
        w_pool = _block_diag(pool_w[l]).astype(BF16)
        p_scale = row2(pool_scale[l])
        w_u = w_up[l].astype(BF16)
        w_d = w_down[l].astype(BF16)
        g_mix, g_ffn = row2(norm_mix[l]), row2(norm_ffn[l])
        sink = attn_sink[l]

        k, v, q, u, f = _inproj(x, mod, g_mix, w_a, cos_t, sin_t, None)
        kc, vc, qc, uc, fc = _inproj(xc, mod, g_mix, w_a, cos_c, sin_c, B)
        o = _attention(q, k, v, kc, vc, sink, True)
        x = _merge(x, o, _pool(u, w_pool, p_scale), _fourier(f, four_lat), mod, g_mix,
                   w_g, w_a_br, w_p_br, w_f_br, w_o, None)
        if not last:
            oc = _attention(qc, None, None, kc, vc, sink, False)
            xc = _merge(xc, oc, _pool(uc, w_pool, p_scale), _fourier(fc, four_ctx), mod, g_mix,
                        w_g, w_a_br, w_p_br, w_f_br, w_o, B)
        x = _ffn(x, mod, g_ffn, w_u, conv_w[l], w_d, gain_final, None, last)
        if not last:
            xc = _ffn(xc, mod, g_ffn, w_u, conv_w[l], w_d, gain_final, B, False)
    return x
```

```python
import functools
import math

import numpy as np
import jax
import jax.numpy as jnp
from jax import lax
from jax.experimental import pallas as pl
from jax.experimental.pallas import tpu as pltpu

D_MODEL = 1024
N_Q_HEADS = 8
N_KV_HEADS = 2
HEAD_DIM = 64
Q_PER_KV = N_Q_HEADS // N_KV_HEADS
ATTN_WIDTH = N_Q_HEADS * HEAD_DIM
KV_WIDTH = N_KV_HEADS * HEAD_DIM
BLOCK = 128
GRID_W = 64
ROPE_BASE = 10000.0
POOL_WIDTH = 256
POOL_GROUP_DIM = 64
FOURIER_WIDTH = 256
FOURIER_GROUP_DIM = 64
D_FF = 2816
N_MOD = 6
EPS = 1e-6
NEG_INF = -1e30

LANES = 128
SUBLANES = 8
VMEM_LIMIT_BYTES = 56 * 1024 * 1024

TOKEN_TILE = 512
FF_CHUNK = 256
MERGE_CHUNK = 256
FOURIER_UNROLL = 4
POOL_HALO = 8
ADALN_TILE = 1536
COND_ROWS = 8

F32 = jnp.float32
BF16 = jnp.bfloat16


def _dot(a, b):
    return jnp.dot(a, b, preferred_element_type=F32)


def _dot_f32(a, b):
    return jnp.dot(a, b, preferred_element_type=F32, precision=lax.Precision.HIGHEST)


def _dot_nt(a, b):
    return lax.dot_general(a, b, (((1,), (1,)), ((), ())), preferred_element_type=F32)


def _modnorm(x, g, sc, sh):
    ms = jnp.mean(x * x, axis=-1, keepdims=True)
    y = x * lax.rsqrt(ms + EPS) * g
    return y * (1.0 + sc) + sh


def _resident(shape):
    nd = len(shape)
    return pl.BlockSpec(shape, lambda *_: (0,) * nd, pipeline_mode=pl.Buffered(1))


def _mod_spec(j, ctx_row):
    if ctx_row is None:
        return pl.BlockSpec((None, None, 1, D_MODEL), lambda b, i: (b, j, 0, 0))
    return pl.BlockSpec((None, None, 1, D_MODEL), lambda b, i: (ctx_row, j, 0, 0))


def _params(*sem):
    return pltpu.CompilerParams(dimension_semantics=sem, vmem_limit_bytes=VMEM_LIMIT_BYTES)


def _adaln_kernel(c_ref, w_ref, b_ref, o_ref):
    c = c_ref[...]
    a = (c * jax.nn.sigmoid(c)).astype(BF16)
    o_ref[...] = _dot(a, w_ref[...].astype(BF16)) + b_ref[...]


def _adaln(cond, w_mod, b_mod):
    depth, _, width = w_mod.shape
    return pl.pallas_call(
        _adaln_kernel,
        grid=(depth, width // ADALN_TILE),
        in_specs=[
            pl.BlockSpec((COND_ROWS, D_MODEL), lambda l, j: (0, 0)),
            pl.BlockSpec((None, D_MODEL, ADALN_TILE), lambda l, j: (l, 0, j)),
            pl.BlockSpec((None, 1, ADALN_TILE), lambda l, j: (l, 0, j)),
        ],
        out_specs=pl.BlockSpec((None, COND_ROWS, ADALN_TILE), lambda l, j: (l, 0, j)),
        out_shape=jax.ShapeDtypeStruct((depth, COND_ROWS, width), F32),
        compiler_params=_params("arbitrary", "arbitrary"),
        name="adaln",
    )(cond, w_mod, b_mod.reshape(depth, 1, width))


_K_COLS = 2 * KV_WIDTH
_A_SPLITS = (0, _K_COLS, _K_COLS + ATTN_WIDTH, _K_COLS + ATTN_WIDTH + POOL_WIDTH,
             _K_COLS + ATTN_WIDTH + POOL_WIDTH + FOURIER_WIDTH,
             _K_COLS + ATTN_WIDTH + POOL_WIDTH + FOURIER_WIDTH + KV_WIDTH)


def _qk_columns():
    quarter = HEAD_DIM // 4
    x1 = np.r_[0:quarter, 2 * quarter:3 * quarter]
    x2 = x1 + quarter
    cols = []
    for h in range(N_KV_HEADS):
        cols += [h * HEAD_DIM + x1, h * HEAD_DIM + x1, h * HEAD_DIM + x2, h * HEAD_DIM + x2]
    q0 = 2 * KV_WIDTH
    for t in range(N_Q_HEADS // 2):
        a, b = q0 + 2 * t * HEAD_DIM, q0 + (2 * t + 1) * HEAD_DIM
        cols += [a + x1, b + x1, a + x2, b + x2]
    return np.concatenate(cols)


def _inproj_kernel(x_ref, g_ref, sh_ref, sc_ref, w_ref, cos_ref, sin_ref,
                   k_ref, v_ref, q_ref, u_ref, f_ref):
    h = _modnorm(x_ref[...], g_ref[...], sc_ref[...], sh_ref[...]).astype(BF16)
    cos = cos_ref[...]
    sin = sin_ref[...]

    def rope(t):
        return t * cos + pltpu.roll(t, LANES // 2, 1) * sin

    k0, q0, u0, f0, v0, end = _A_SPLITS
    tile = lambda a, j: a[:, j * LANES:(j + 1) * LANES]
    kq = _dot(h, w_ref[:, k0:u0])
    for j in range(_K_COLS // LANES):
        k_ref[:, j * LANES:(j + 1) * LANES] = rope(tile(kq, j)).astype(BF16)
    scale = HEAD_DIM ** -0.5
    for j in range(ATTN_WIDTH // LANES):
        q_ref[:, j * LANES:(j + 1) * LANES] = (rope(tile(kq, _K_COLS // LANES + j)) * scale).astype(BF16)
    ufv = _dot(h, w_ref[:, u0:end])
    u_ref[...] = ufv[:, 0:f0 - u0]
    for j in range(FOURIER_WIDTH // LANES):
        f_ref[j] = tile(ufv, (f0 - u0) // LANES + j)
    v_ref[...] = ufv[:, v0 - u0:end - u0].astype(BF16)


def _inproj(x, mod, gain, w_a, cos_t, sin_t, ctx_row):
    B, T, _ = x.shape
    tm = min(TOKEN_TILE, T)
    tok = lambda w: pl.BlockSpec((None, tm, w), lambda b, i: (b, i, 0))
    return pl.pallas_call(
        _inproj_kernel,
        grid=(B, T // tm),
        in_specs=[
            tok(D_MODEL),
            _resident((1, D_MODEL)),
            _mod_spec(0, ctx_row), _mod_spec(1, ctx_row),
            _resident(w_a.shape),
            pl.BlockSpec((tm, LANES), lambda b, i: (i, 0)),
            pl.BlockSpec((tm, LANES), lambda b, i: (i, 0)),
        ],
        out_specs=[tok(_K_COLS), tok(KV_WIDTH), tok(ATTN_WIDTH), tok(POOL_WIDTH),
                   pl.BlockSpec((None, FOURIER_WIDTH // LANES, tm, LANES), lambda b, i: (b, 0, i, 0))],
        out_shape=[
            jax.ShapeDtypeStruct((B, T, _K_COLS), BF16),
            jax.ShapeDtypeStruct((B, T, KV_WIDTH), BF16),
            jax.ShapeDtypeStruct((B, T, ATTN_WIDTH), BF16),
            jax.ShapeDtypeStruct((B, T, POOL_WIDTH), F32),
            jax.ShapeDtypeStruct((B, FOURIER_WIDTH // LANES, T, LANES), F32),
        ],
        compiler_params=_params("arbitrary", "arbitrary"),
        name="inproj",
    )(x, gain, mod, mod, w_a, cos_t, sin_t)


def _attn_kernel(sink_ref, q_ref, *refs, nb, band):
    if band:
        kp_ref, km_ref, kn_ref, vp_ref, vm_ref, vn_ref, kc_ref, vc_ref, o_ref = refs
    else:
        kc_ref, vc_ref, o_ref = refs
    n = pl.program_id(1)
    rows = Q_PER_KV * BLOCK
    lane = lax.broadcasted_iota(jnp.int32, (BLOCK, LANES), 1)
    first_head = (lane % HEAD_DIM) < HEAD_DIM // 2
    low = lane < HEAD_DIM
    rgrp = lax.broadcasted_iota(jnp.int32, (rows, 1), 0) // BLOCK
    if band:
        qi = lax.broadcasted_iota(jnp.int32, (rows, BLOCK), 0) % BLOCK
        kj = lax.broadcasted_iota(jnp.int32, (rows, BLOCK), 1)
        ok_prev = jnp.logical_and(kj >= qi, n > 0)
        ok_next = jnp.logical_and(kj <= qi, n < nb - 1)
        v_all = jnp.concatenate([vp_ref[...], vm_ref[...], vn_ref[...], vc_ref[...]], axis=0)
    else:
        v_all = vc_ref[...]
    outs = []
    for h in range(N_KV_HEADS):
        qs = []
        for t in range(Q_PER_KV // 2):
            c0 = (h * (Q_PER_KV // 2) + t) * LANES
            qt = q_ref[:, c0:c0 + LANES]
            qs.append(jnp.where(first_head, qt, jnp.zeros_like(qt)))
            qs.append(jnp.where(first_head, jnp.zeros_like(qt), qt))
        qh = jnp.concatenate(qs, axis=0)
        ks = slice(h * LANES, (h + 1) * LANES)
        if band:
            k_all = jnp.concatenate([kp_ref[:, ks], km_ref[:, ks], kn_ref[:, ks], kc_ref[:, ks]], axis=0)
        else:
            k_all = kc_ref[:, ks]
        s = _dot_nt(qh, k_all)
        tiles = [s[:, j * LANES:(j + 1) * LANES] for j in range(s.shape[1] // LANES)]
        if band:
            tiles[0] = jnp.where(ok_prev, tiles[0], NEG_INF)
            tiles[2] = jnp.where(ok_next, tiles[2], NEG_INF)
        sink = jnp.zeros((rows, 1), F32)
        for g in range(Q_PER_KV):
            sink = jnp.where(rgrp == g, sink_ref[h * Q_PER_KV + g], sink)
        m_t = tiles[0]
        for t in tiles[1:]:
            m_t = jnp.maximum(m_t, t)
        m = jnp.maximum(jnp.max(m_t, axis=-1, keepdims=True), sink)
        es = [jnp.exp(t - m) for t in tiles]
        l_t = es[0]
        for e in es[1:]:
            l_t = l_t + e
        denom = jnp.sum(l_t, axis=-1, keepdims=True) + jnp.exp(sink - m)
        p = jnp.concatenate([e.astype(BF16) for e in es], axis=1)
        outs.append(_dot(p, v_all) / denom)
    for g in range(Q_PER_KV):
        r = slice(g * BLOCK, (g + 1) * BLOCK)
        o_ref[:, g * LANES:(g + 1) * LANES] = jnp.where(low, outs[0][r], outs[1][r]).astype(BF16)


def _attention(q, k, v, kc, vc, sink, band):
    B, T, _ = q.shape
    L = kc.shape[1]
    nb = T // BLOCK
    qspec = pl.BlockSpec((None, BLOCK, ATTN_WIDTH), lambda b, n: (b, n, 0))
    kprev = lambda w: pl.BlockSpec((None, BLOCK, w), lambda b, n: (b, jnp.maximum(n - 1, 0), 0))
    kmid = lambda w: pl.BlockSpec((None, BLOCK, w), lambda b, n: (b, n, 0))
    knext = lambda w: pl.BlockSpec((None, BLOCK, w), lambda b, n: (b, jnp.minimum(n + 1, nb - 1), 0))
    cspec = lambda w: pl.BlockSpec((None, L, w), lambda b, n: (b, 0, 0))
    in_specs = [pl.BlockSpec(memory_space=pltpu.SMEM), qspec]
    args = [sink, q]
    if band:
        in_specs += [kprev(_K_COLS), kmid(_K_COLS), knext(_K_COLS),
                     kprev(KV_WIDTH), kmid(KV_WIDTH), knext(KV_WIDTH)]
        args += [k, k, k, v, v, v]
    in_specs += [cspec(_K_COLS), cspec(KV_WIDTH)]
    args += [kc, vc]
    return pl.pallas_call(
        functools.partial(_attn_kernel, nb=nb, band=band),
        grid=(B, nb),
        in_specs=in_specs,
        out_specs=pl.BlockSpec((None, BLOCK, ATTN_WIDTH), lambda b, n: (b, n, 0)),
        out_shape=jax.ShapeDtypeStruct((B, T, ATTN_WIDTH), BF16),
        compiler_params=_params("arbitrary", "arbitrary"),
        name="attention_band" if band else "attention_ctx",
    )(*args)


def _pool_kernel(u_ref, w_ref, s_ref, o_ref, pad_ref, *, T, tc):
    zeros = jnp.zeros((POOL_HALO, POOL_WIDTH), F32)
    pad_ref[0:POOL_HALO, :] = zeros
    pad_ref[POOL_HALO + T:POOL_HALO + T + POOL_HALO, :] = zeros
    pad_ref[POOL_HALO:POOL_HALO + T, :] = u_ref[...]
    lane = lax.broadcasted_iota(jnp.int32, (tc, LANES), 1)
    upper = lane >= POOL_GROUP_DIM
    row = lax.broadcasted_iota(jnp.int32, (tc, LANES), 0)
    for c in range(T // tc):
        base = POOL_HALO + c * tc
        t = row + c * tc
        pooled = []
        for tile in range(POOL_WIDTH // LANES):
            w_small = 2 << (2 * tile)
            w_big = 2 * w_small
            cols = slice(tile * LANES, (tile + 1) * LANES)
            load = lambda d: pad_ref[base + d:base + d + tc, cols]
            inner = load(-(w_small // 2))
            for d in range(-(w_small // 2) + 1, w_small // 2):
                inner = inner + load(d)
            outer = load(-(w_big // 2))
            for d in list(range(-(w_big // 2) + 1, -(w_small // 2))) + list(range(w_small // 2, w_big // 2)):
                outer = outer + load(d)
            win = inner + jnp.where(upper, outer, 0.0)
            half = jnp.where(upper, w_big // 2, w_small // 2)
            lo = jnp.maximum(t - half, 0)
            hi = jnp.minimum(t + half, T)
            cnt = (hi - lo).astype(F32)
            pooled.append(win / cnt - load(0))
        p = jnp.concatenate(pooled, axis=1).astype(BF16)
        y = _dot(p, w_ref[...]) * s_ref[...]
        o_ref[c * tc:(c + 1) * tc, :] = y.astype(BF16)


def _pool(u, w_bd, scale):
    B, T, _ = u.shape
    tc = min(TOKEN_TILE, T)
    return pl.pallas_call(
        functools.partial(_pool_kernel, T=T, tc=tc),
        grid=(B,),
        in_specs=[
            pl.BlockSpec((None, T, POOL_WIDTH), lambda b: (b, 0, 0)),
            _resident(w_bd.shape),
            _resident((1, POOL_WIDTH)),
        ],
        out_specs=pl.BlockSpec((None, T, POOL_WIDTH), lambda b: (b, 0, 0)),
        out_shape=jax.ShapeDtypeStruct((B, T, POOL_WIDTH), BF16),
        scratch_shapes=[pltpu.VMEM((T + 2 * POOL_HALO, POOL_WIDTH), F32)],
        compiler_params=_params("arbitrary"),
        name="pool",
    )(u, w_bd, scale)


def _fourier_factors(T):
    n1 = 1 << (int(math.log2(T)) // 2)
    return n1, T // n1


def _split_const(a):
    a = jnp.asarray(a, F32)
    hi = a.astype(BF16)
    return hi, (a - hi.astype(F32)).astype(BF16)


def _fourier_tables(T):
    n1, n2 = _fourier_factors(T)
    gd = FOURIER_GROUP_DIM
    t2 = np.arange(n2)[:, None, None]
    k1 = np.arange(n1)[None, :, None]
    t1 = np.arange(n1)[None, None, :]
    ph = -2.0 * np.pi * (t2 * k1 / T + t1 * k1 / n1)
    m1 = np.concatenate([np.cos(ph), np.sin(ph)], axis=1) / math.sqrt(n1)
    a3 = 2.0 * np.pi * np.outer(np.arange(n2), np.arange(n2)) / n2
    c3, s3 = np.cos(a3) / math.sqrt(n2), np.sin(a3) / math.sqrt(n2)
    m3 = np.block([[c3, s3], [-s3, c3]])
    c = np.arange(gd)
    ang = 2.0 * np.pi * np.outer(c, c) / gd
    eye = np.eye(FOURIER_WIDTH // gd)
    cd = np.concatenate([np.kron(eye, np.cos(ang)), np.kron(eye, np.sin(ang))], axis=0) / math.sqrt(gd)
    return _split_const(m1) + _split_const(m3) + _split_const(cd)


def _split(a):
    hi = a.astype(BF16)
    return hi, (a - hi.astype(F32)).astype(BF16)


def _dot3(a, b):
    return _dot(a[0], b[0]) + (_dot(a[1], b[0]) + _dot(a[0], b[1]))


def _fourier_kernel(f_ref, m1h_ref, m1l_ref, m3h_ref, m3l_ref, cdh_ref, cdl_ref, o_ref, a_ref, g_ref,
                    *, n1, n2, tc, unroll):
    T = n1 * n2
    wt = FOURIER_WIDTH // LANES
    lanes = lambda j: slice(j * LANES, (j + 1) * LANES)

    def over_t1(t2, carry):
        rows = pl.ds(t2, n1, stride=n2)
        x = jnp.concatenate([f_ref[j, rows, :] for j in range(wt)], axis=1)
        a = _dot3((m1h_ref[t2], m1l_ref[t2]), _split(x))
        off = pl.multiple_of(t2 * n1, n1)
        for j in range(wt):
            a_ref[j, pl.ds(off, n1), :] = a[:n1, lanes(j)]
            a_ref[wt + j, pl.ds(off, n1), :] = a[n1:, lanes(j)]
        return carry

    lax.fori_loop(0, n2, over_t1, 0, unroll=unroll)

    def over_t2(k1, carry):
        rows = pl.ds(k1, n2, stride=n1)
        b = jnp.concatenate(
            [jnp.concatenate([a_ref[h * wt + j, rows, :] for j in range(wt)], axis=1) for h in range(2)], axis=0)
        g = _dot3((m3h_ref[...], m3l_ref[...]), _split(b))
        for j in range(wt):
            g_ref[j, rows, :] = g[:n2, lanes(j)]
            g_ref[wt + j, rows, :] = g[n2:, lanes(j)]
        return carry

    lax.fori_loop(0, n1, over_t2, 0, unroll=unroll)

    for c in range(T // tc):
        r = slice(c * tc, (c + 1) * tc)
        g = jnp.concatenate([g_ref[j, r, :] for j in range(2 * wt)], axis=1)
        o_ref[r, :] = _dot3(_split(g), (cdh_ref[...], cdl_ref[...])).astype(BF16)


def _fourier(f, tables):
    B, wt, T, _ = f.shape
    n1, n2 = _fourier_factors(T)
    return pl.pallas_call(
        functools.partial(_fourier_kernel, n1=n1, n2=n2, tc=min(TOKEN_TILE, T), unroll=FOURIER_UNROLL),
        grid=(B,),
        in_specs=[pl.BlockSpec((None, wt, T, LANES), lambda b: (b, 0, 0, 0))] + [_resident(t.shape) for t in tables],
        out_specs=pl.BlockSpec((None, T, FOURIER_WIDTH), lambda b: (b, 0, 0)),
        out_shape=jax.ShapeDtypeStruct((B, T, FOURIER_WIDTH), BF16),
        scratch_shapes=[pltpu.VMEM((2 * wt, T, LANES), F32), pltpu.VMEM((2 * wt, T, LANES), F32)],
        compiler_params=_params("arbitrary"),
        name="fourier",
    )(f, *tables)


def _merge_kernel(x_ref, o_ref, p_ref, f_ref, g_ref, sh_ref, sc_ref, gt_ref,
                  wg_ref, wa_ref, wp_ref, wf_ref, wo_ref, out_ref, y_ref):
    x = x_ref[...]
    h = _modnorm(x, g_ref[...], sc_ref[...], sh_ref[...]).astype(BF16)
    o = o_ref[...]
    p = p_ref[...]
    f = f_ref[...]
    for j in range(D_MODEL // MERGE_CHUNK):
        c = slice(j * MERGE_CHUNK, (j + 1) * MERGE_CHUNK)
        gate = lambda b: jax.nn.sigmoid(_dot(h, wg_ref[:, b * D_MODEL + j * MERGE_CHUNK:
                                                        b * D_MODEL + (j + 1) * MERGE_CHUNK]))
        y = (gate(0) * _dot(o, wa_ref[:, c]) + gate(1) * _dot(p, wp_ref[:, c])
             + gate(2) * _dot(f, wf_ref[:, c]))
        y_ref[:, c] = y.astype(BF16)
    out_ref[...] = x + gt_ref[...] * _dot(y_ref[...], wo_ref[...])


def _merge(x, o, p, f, mod, gain, wg, wa, wp, wf, wo, ctx_row):
    B, T, _ = x.shape
    tm = min(TOKEN_TILE, T)
    tok = lambda w: pl.BlockSpec((None, tm, w), lambda b, i: (b, i, 0))
    return pl.pallas_call(
        _merge_kernel,
        grid=(B, T // tm),
        in_specs=[
            tok(D_MODEL), tok(ATTN_WIDTH), tok(POOL_WIDTH), tok(FOURIER_WIDTH),
            _resident((1, D_MODEL)),
            _mod_spec(0, ctx_row), _mod_spec(1, ctx_row), _mod_spec(2, ctx_row),
            _resident(wg.shape), _resident(wa.shape), _resident(wp.shape), _resident(wf.shape),
            _resident(wo.shape),
        ],
        out_specs=tok(D_MODEL),
        out_shape=jax.ShapeDtypeStruct((B, T, D_MODEL), F32),
        scratch_shapes=[pltpu.VMEM((tm, D_MODEL), BF16)],
        compiler_params=_params("arbitrary", "arbitrary"),
        name="merge",
    )(x, o, p, f, gain, mod, mod, mod, wg, wa, wp, wf, wo)


def _ffn_kernel(x_ref, xp_ref, xn_ref, g_ref, sh_ref, sc_ref, gt_ref, wu_ref, cw_ref, wd_ref, gf_ref,
                out_ref, h_ref, a_ref, *, tm, final):
    i = pl.program_id(1)
    last = pl.num_programs(1) - 1
    g, sh, sc = g_ref[...], sh_ref[...], sc_ref[...]
    x = x_ref[...]
    H = SUBLANES
    h_ref[H:H + tm, :] = _modnorm(x, g, sc, sh).astype(BF16)
    keep_prev = (i > 0).astype(F32)
    keep_next = (i < last).astype(F32)
    h_ref[0:H, :] = (_modnorm(xp_ref[...], g, sc, sh) * keep_prev).astype(BF16)
    h_ref[H + tm:H + tm + H, :] = (_modnorm(xn_ref[...], g, sc, sh) * keep_next).astype(BF16)
    hf = h_ref[...]
    rows = tm + 2 * H

    def conv(up, cols):
        w = cw_ref[:, cols]
        y = (pltpu.roll(up, 1, 0) * w[0:1] + up * w[1:2] + pltpu.roll(up, rows - 1, 0) * w[2:3])
        return y[H:H + tm]

    for j in range(D_FF // FF_CHUNK):
        cv = slice(j * FF_CHUNK, (j + 1) * FF_CHUNK)
        cg = slice(D_FF + j * FF_CHUNK, D_FF + (j + 1) * FF_CHUNK)
        val = conv(_dot(hf, wu_ref[:, cv]), cv)
        gate = conv(_dot(hf, wu_ref[:, cg]), cg)
        a_ref[:, cv] = (val * (gate * jax.nn.sigmoid(gate))).astype(BF16)
    y = x + gt_ref[...] * _dot(a_ref[...], wd_ref[...])
    if final:
        ms = jnp.mean(y * y, axis=-1, keepdims=True)
        y = y * lax.rsqrt(ms + EPS) * gf_ref[...]
    out_ref[...] = y


def _ffn(x, mod, gain, wu, cw, wd, gain_final, ctx_row, final):
    B, T, _ = x.shape
    tm = min(TOKEN_TILE, T)
    per = tm // SUBLANES
    nblk = T // SUBLANES
    tok = pl.BlockSpec((None, tm, D_MODEL), lambda b, i: (b, i, 0))
    prev = pl.BlockSpec((None, SUBLANES, D_MODEL), lambda b, i: (b, jnp.maximum(i * per - 1, 0), 0))
    nxt = pl.BlockSpec((None, SUBLANES, D_MODEL), lambda b, i: (b, jnp.minimum((i + 1) * per, nblk - 1), 0))
    return pl.pallas_call(
        functools.partial(_ffn_kernel, tm=tm, final=final),
        grid=(B, T // tm),
        in_specs=[
            tok, prev, nxt,
            _resident((1, D_MODEL)),
            _mod_spec(3, ctx_row), _mod_spec(4, ctx_row), _mod_spec(5, ctx_row),
            _resident(wu.shape), _resident(cw.shape), _resident(wd.shape),
            _resident((1, D_MODEL)),
        ],
        out_specs=tok,
        out_shape=jax.ShapeDtypeStruct((B, T, D_MODEL), F32),
        scratch_shapes=[pltpu.VMEM((tm + 2 * SUBLANES, D_MODEL), BF16), pltpu.VMEM((tm, D_FF), BF16)],
        compiler_params=_params("arbitrary", "arbitrary"),
        name="ffn_final" if final else "ffn",
    )(x, x, x, gain, mod, mod, mod, wu, cw, wd, gain_final)


def _rope_tables(T):
    rows = T // GRID_W
    row = jnp.repeat(jnp.arange(rows, dtype=jnp.int32), GRID_W).astype(F32)
    col = jnp.tile(jnp.arange(GRID_W, dtype=jnp.int32), rows).astype(F32)
    n_freq = HEAD_DIM // 4
    inv_freq = ROPE_BASE ** (-jnp.arange(n_freq, dtype=F32) / n_freq)
    ar = row[:, None] * inv_freq[None, :]
    ac = col[:, None] * inv_freq[None, :]
    cos = jnp.concatenate([jnp.cos(ar), jnp.cos(ac)], axis=1)
    sin = jnp.concatenate([jnp.sin(ar), jnp.sin(ac)], axis=1)
    return jnp.tile(cos, (1, 4)), jnp.concatenate([-sin, -sin, sin, sin], axis=1)


def _block_diag(w):
    g, a, b = w.shape
    out = jnp.zeros((g * a, g * b), w.dtype)
    for i in range(g):
        out = out.at[i * a:(i + 1) * a, i * b:(i + 1) * b].set(w[i])
    return out


def kernel(x, c, ctx, c_ctx, w_mod, b_mod, norm_mix, norm_ffn, w_in, attn_sink, pool_w, pool_scale,
           w_br_attn, w_br_pool, w_br_four, w_out, w_up, conv_w, w_down, norm_final):
    B, T, _ = x.shape
    L = ctx.shape[1]
    depth = w_mod.shape[0]
    assert B < COND_ROWS and T % TOKEN_TILE == 0 and T % BLOCK == 0 and L % BLOCK == 0

    cond = jnp.zeros((COND_ROWS, D_MODEL), F32).at[:B].set(c).at[B].set(c_ctx)
    mods = _adaln(cond, w_mod, b_mod)

    cos_t, sin_t = _rope_tables(T)
    cos_c = jnp.ones((L, LANES), F32)
    sin_c = jnp.zeros((L, LANES), F32)
    qk_cols = jnp.asarray(_qk_columns(), jnp.int32)
    four_lat = _fourier_tables(T)
    four_ctx = _fourier_tables(L)
    row2 = lambda v: v.reshape(1, -1)
    gain_final = row2(norm_final)

    xc = ctx
    for l in range(depth):
        last = l == depth - 1
        mod = mods[l].reshape(COND_ROWS, N_MOD, 1, D_MODEL)
        wl = w_in[l]
        n_qkv = 2 * KV_WIDTH + ATTN_WIDTH
        n_tok = n_qkv + POOL_WIDTH + FOURIER_WIDTH
        w_a = jnp.concatenate([jnp.take(wl, qk_cols, axis=1), wl[:, n_qkv:n_tok], wl[:, KV_WIDTH:2 * KV_WIDTH]],
                              axis=1).astype(BF16)
        w_g = wl[:, n_tok:].astype(BF16)
        w_a_br = (w_br_attn[l].reshape(N_KV_HEADS, Q_PER_KV, HEAD_DIM, D_MODEL)
                  .transpose(1, 0, 2, 3).reshape(ATTN_WIDTH, D_MODEL).astype(BF16))
        w_p_br = w_br_pool[l].astype(BF16)
        w_f_br = w_br_four[l].astype(BF16)
        w_o = w_out[l].astype(BF16)
        w_pool = _block_diag(pool_w[l]).astype(BF16)
        p_scale = row2(pool_scale[l])
        w_u = w_up[l].astype(BF16)
        w_d = w_down[l].astype(BF16)
        g_mix, g_ffn = row2(norm_mix[l]), row2(norm_ffn[l])
        sink = attn_sink[l]

        k, v, q, u, f = _inproj(x, mod, g_mix, w_a, cos_t, sin_t, None)
        kc, vc, qc, uc, fc = _inproj(xc, mod, g_mix, w_a, cos_c, sin_c, B)
        o = _attention(q, k, v, kc, vc, sink, True)
        x = _merge(x, o, _pool(u, w_pool, p_scale), _fourier(f, four_lat), mod, g_mix,
                   w_g, w_a_br, w_p_br, w_f_br, w_o, None)
        if not last:
            oc = _attention(qc, None, None, kc, vc, sink, False)
            xc = _merge(xc, oc, _pool(uc, w_pool, p_scale), _fourier(fc, four_ctx), mod, g_mix,
                        w_g, w_a_br, w_p_br, w_f_br, w_o, B)
        x = _ffn(x, mod, g_ffn, w_u, conv_w[l], w_d, gain_final, None, last)
        if not last:
            xc = _ffn(xc, mod, g_ffn, w_u, conv_w[l], w_d, gain_final, B, False)
    return x
```

```python
import functools
import math

import numpy as np
import jax
import jax.numpy as jnp
from jax import lax
from jax.experimental import pallas as pl
from jax.experimental.pallas import tpu as pltpu

D_MODEL = 1024
N_Q_HEADS = 8
N_KV_HEADS = 2
HEAD_DIM = 64
Q_PER_KV = N_Q_HEADS // N_KV_HEADS
ATTN_WIDTH = N_Q_HEADS * HEAD_DIM
KV_WIDTH = N_KV_HEADS * HEAD_DIM
BLOCK = 128
GRID_W = 64
ROPE_BASE = 10000.0
POOL_WIDTH = 256
POOL_GROUP_DIM = 64
FOURIER_WIDTH = 256
FOURIER_GROUP_DIM = 64
D_FF = 2816
N_MOD = 6
EPS = 1e-6
NEG_INF = -1e30
LOG2E = math.log2(math.e)

LANES = 128
SUBLANES = 8
VMEM_LIMIT_BYTES = 56 * 1024 * 1024

TOKEN_TILE = 512
MERGE_TILE = 1024
FFN_TILE = 1024
ATTN_Q_BLOCKS = 2
FF_CHUNK = 256
MERGE_CHUNK = 256
FOURIER_UNROLL = 4
POOL_HALO = 8
ADALN_TILE = 1536
COND_ROWS = 8

F32 = jnp.float32
BF16 = jnp.bfloat16


def _dot(a, b):
    return jnp.dot(a, b, preferred_element_type=F32)


def _dot_nt(a, b):
    return lax.dot_general(a, b, (((1,), (1,)), ((), ())), preferred_element_type=F32)


def _modnorm(x, g, sc, sh):
    ms = jnp.mean(x * x, axis=-1, keepdims=True)
    y = x * lax.rsqrt(ms + EPS) * g
    return y * (1.0 + sc) + sh


def _resident(shape):
    nd = len(shape)
    return pl.BlockSpec(shape, lambda *_: (0,) * nd, pipeline_mode=pl.Buffered(1))


def _layer(arr, l):
    nd = arr.ndim - 1
    return pl.BlockSpec((None,) + arr.shape[1:], lambda *_: (l,) + (0,) * nd, pipeline_mode=pl.Buffered(1))


def _mod_spec(l, j, ctx_row):
    if ctx_row is None:
        return pl.BlockSpec((None, None, None, 1, D_MODEL), lambda b, i: (l, b, j, 0, 0))
    return pl.BlockSpec((None, None, None, 1, D_MODEL), lambda b, i: (l, ctx_row, j, 0, 0))


def _params(*sem):
    return pltpu.CompilerParams(dimension_semantics=sem, vmem_limit_bytes=VMEM_LIMIT_BYTES)


def _adaln_kernel(c_ref, w_ref, b_ref, o_ref):
    c = c_ref[...]
    a = (c * jax.nn.sigmoid(c)).astype(BF16)
    o_ref[...] = _dot(a, w_ref[...].astype(BF16)) + b_ref[...]


def _adaln(cond, w_mod, b_mod):
    depth, _, width = w_mod.shape
    return pl.pallas_call(
        _adaln_kernel,
        grid=(depth, width // ADALN_TILE),
        in_specs=[
            pl.BlockSpec((COND_ROWS, D_MODEL), lambda l, j: (0, 0)),
            pl.BlockSpec((None, D_MODEL, ADALN_TILE), lambda l, j: (l, 0, j)),
            pl.BlockSpec((None, 1, ADALN_TILE), lambda l, j: (l, 0, j)),
        ],
        out_specs=pl.BlockSpec((None, COND_ROWS, ADALN_TILE), lambda l, j: (l, 0, j)),
        out_shape=jax.ShapeDtypeStruct((depth, COND_ROWS, width), F32),
        compiler_params=_params("arbitrary", "arbitrary"),
        name="adaln",
    )(cond, w_mod, b_mod.reshape(depth, 1, width))


_K_COLS = 2 * KV_WIDTH
_A_SPLITS = (0, _K_COLS, _K_COLS + ATTN_WIDTH, _K_COLS + ATTN_WIDTH + POOL_WIDTH,
             _K_COLS + ATTN_WIDTH + POOL_WIDTH + FOURIER_WIDTH,
             _K_COLS + ATTN_WIDTH + POOL_WIDTH + FOURIER_WIDTH + KV_WIDTH)


def _qk_columns():
    quarter = HEAD_DIM // 4
    x1 = np.r_[0:quarter, 2 * quarter:3 * quarter]
    x2 = x1 + quarter
    cols = []
    for h in range(N_KV_HEADS):
        cols += [h * HEAD_DIM + x1, h * HEAD_DIM + x1, h * HEAD_DIM + x2, h * HEAD_DIM + x2]
    q0 = 2 * KV_WIDTH
    for t in range(N_Q_HEADS // 2):
        a, b = q0 + 2 * t * HEAD_DIM, q0 + (2 * t + 1) * HEAD_DIM
        cols += [a + x1, b + x1, a + x2, b + x2]
    return np.concatenate(cols)


def _inproj_kernel(x_ref, g_ref, sh_ref, sc_ref, w_ref, cos_ref, sin_ref,
                   k_ref, v_ref, q_ref, u_ref, f_ref):
    h = _modnorm(x_ref[...], g_ref[...], sc_ref[...], sh_ref[...]).astype(BF16)
    cos = cos_ref[...]
    sin = sin_ref[...]

    def rope(t):
        return t * cos + pltpu.roll(t, LANES // 2, 1) * sin

    k0, q0, u0, f0, v0, end = _A_SPLITS
    tile = lambda a, j: a[:, j * LANES:(j + 1) * LANES]
    kq = _dot(h, w_ref[:, k0:u0])
    for j in range(_K_COLS // LANES):
        k_ref[:, j * LANES:(j + 1) * LANES] = rope(tile(kq, j)).astype(BF16)
    scale = HEAD_DIM ** -0.5 * LOG2E
    for j in range(ATTN_WIDTH // LANES):
        q_ref[:, j * LANES:(j + 1) * LANES] = (rope(tile(kq, _K_COLS // LANES + j)) * scale).astype(BF16)
    ufv = _dot(h, w_ref[:, u0:end])
    u_ref[...] = ufv[:, 0:f0 - u0]
    for j in range(FOURIER_WIDTH // LANES):
        f_ref[j] = tile(ufv, (f0 - u0) // LANES + j)
    v_ref[...] = ufv[:, v0 - u0:end - u0].astype(BF16)


def _inproj(l, x, mod, gain, w_a, cos_t, sin_t, ctx_row):
    B, T, _ = x.shape
    tm = min(TOKEN_TILE, T)
    tok = lambda w: pl.BlockSpec((None, tm, w), lambda b, i: (b, i, 0))
    return pl.pallas_call(
        _inproj_kernel,
        grid=(B, T // tm),
        in_specs=[
            tok(D_MODEL),
            _layer(gain, l),
            _mod_spec(l, 0, ctx_row), _mod_spec(l, 1, ctx_row),
            _layer(w_a, l),
            pl.BlockSpec((tm, LANES), lambda b, i: (i, 0)),
            pl.BlockSpec((tm, LANES), lambda b, i: (i, 0)),
        ],
        out_specs=[tok(_K_COLS), tok(KV_WIDTH), tok(ATTN_WIDTH), tok(POOL_WIDTH),
                   pl.BlockSpec((None, FOURIER_WIDTH // LANES, tm, LANES), lambda b, i: (b, 0, i, 0))],
        out_shape=[
            jax.ShapeDtypeStruct((B, T, _K_COLS), BF16),
            jax.ShapeDtypeStruct((B, T, KV_WIDTH), BF16),
            jax.ShapeDtypeStruct((B, T, ATTN_WIDTH), BF16),
            jax.ShapeDtypeStruct((B, T, POOL_WIDTH), F32),
            jax.ShapeDtypeStruct((B, FOURIER_WIDTH // LANES, T, LANES), F32),
        ],
        compiler_params=_params("arbitrary", "arbitrary"),
        name="inproj",
    )(x, gain, mod, mod, w_a, cos_t, sin_t)


def _attn_kernel(sink_ref, q_ref, *refs, layer, steps, band):
    if band:
        kp_ref, km_ref, kn_ref, vp_ref, vm_ref, vn_ref, kc_ref, vc_ref, o_ref = refs
    else:
        kc_ref, vc_ref, o_ref = refs
    n = pl.program_id(1)
    rows = Q_PER_KV * BLOCK
    lane = lax.broadcasted_iota(jnp.int32, (BLOCK, LANES), 1)
    first_head = (lane % HEAD_DIM) < HEAD_DIM // 2
    low = lane < HEAD_DIM
    rgrp = lax.broadcasted_iota(jnp.int32, (rows, 1), 0) // BLOCK
    if band:
        qi = lax.broadcasted_iota(jnp.int32, (rows, BLOCK), 0) % BLOCK
        kj = lax.broadcasted_iota(jnp.int32, (rows, BLOCK), 1)
        in_prev = kj >= qi
        in_next = kj <= qi
    blk = lambda ref, j, cols: ref[j * BLOCK:(j + 1) * BLOCK, cols]
    every = slice(None)

    def one_head(q0, h, k_parts, v_all, ok_prev, ok_next):
        qs = []
        for t in range(Q_PER_KV // 2):
            c0 = (h * (Q_PER_KV // 2) + t) * LANES
            qt = q_ref[q0:q0 + BLOCK, c0:c0 + LANES]
            qs.append(jnp.where(first_head, qt, jnp.zeros_like(qt)))
            qs.append(jnp.where(first_head, jnp.zeros_like(qt), qt))
        qh = jnp.concatenate(qs, axis=0)
        s = _dot_nt(qh, jnp.concatenate(k_parts, axis=0))
        tiles = [s[:, j * LANES:(j + 1) * LANES] for j in range(s.shape[1] // LANES)]
        if band:
            tiles[0] = jnp.where(ok_prev, tiles[0], NEG_INF)
            tiles[2] = jnp.where(ok_next, tiles[2], NEG_INF)
        sink = jnp.zeros((rows, 1), F32)
        for g in range(Q_PER_KV):
            sink = jnp.where(rgrp == g, sink_ref[layer, h * Q_PER_KV + g] * LOG2E, sink)
        m_t = tiles[0]
        for t in tiles[1:]:
            m_t = jnp.maximum(m_t, t)
        m = jnp.maximum(jnp.max(m_t, axis=-1, keepdims=True), sink)
        p = jnp.concatenate([jnp.exp2(t - m).astype(BF16) for t in tiles], axis=1)
        return _dot(p, v_all), jnp.exp2(sink - m)

    n_sub = ATTN_Q_BLOCKS if band else 1
    for sub in range(n_sub):
        if band:
            pick = lambda pm, mid, nx, j, cols: (blk(pm, 0, cols) if j < 0 else
                                                 blk(nx, 0, cols) if j >= n_sub else blk(mid, j, cols))
            ok_prev = jnp.logical_and(in_prev, n > 0) if sub == 0 else in_prev
            ok_next = jnp.logical_and(in_next, n < steps - 1) if sub == n_sub - 1 else in_next
            v_all = jnp.concatenate([pick(vp_ref, vm_ref, vn_ref, sub + d, every) for d in (-1, 0, 1)]
                                    + [vc_ref[...]], axis=0)
        else:
            ok_prev = ok_next = None
            v_all = vc_ref[...]
        v_low = lax.broadcasted_iota(jnp.int32, v_all.shape, 1) < HEAD_DIM
        ones = jnp.ones_like(v_all)
        v_aug = [jnp.where(v_low, v_all, ones), jnp.where(v_low, ones, v_all)]
        outs = []
        for h in range(N_KV_HEADS):
            ks = slice(h * LANES, (h + 1) * LANES)
            k_parts = [kc_ref[:, ks]]
            if band:
                k_parts = [pick(kp_ref, km_ref, kn_ref, sub + d, ks) for d in (-1, 0, 1)] + k_parts
            outs.append(one_head(sub * BLOCK, h, k_parts, v_aug[h], ok_prev, ok_next))
        (o0, e0), (o1, e1) = outs
        for g in range(Q_PER_KV):
            r = slice(g * BLOCK, (g + 1) * BLOCK)
            num = jnp.where(low, o0[r], o1[r])
            den = pltpu.roll(jnp.where(low, o1[r], o0[r]), HEAD_DIM, 1) + jnp.where(low, e0[r], e1[r])
            o_ref[sub * BLOCK:(sub + 1) * BLOCK, g * LANES:(g + 1) * LANES] = (num / den).astype(BF16)


def _attention(l, q, k, v, kc, vc, sink, band):
    B, T, _ = q.shape
    L = kc.shape[1]
    nq = ATTN_Q_BLOCKS if band else 1
    rows = nq * BLOCK
    steps = T // rows
    nb = T // BLOCK
    qspec = pl.BlockSpec((None, rows, ATTN_WIDTH), lambda b, n: (b, n, 0))
    kprev = lambda w: pl.BlockSpec((None, BLOCK, w), lambda b, n: (b, jnp.maximum(n * nq - 1, 0), 0))
    kmid = lambda w: pl.BlockSpec((None, rows, w), lambda b, n: (b, n, 0))
    knext = lambda w: pl.BlockSpec((None, BLOCK, w), lambda b, n: (b, jnp.minimum((n + 1) * nq, nb - 1), 0))
    cspec = lambda w: pl.BlockSpec((None, L, w), lambda b, n: (b, 0, 0))
    in_specs = [pl.BlockSpec(memory_space=pltpu.SMEM), qspec]
    args = [sink, q]
    if band:
        in_specs += [kprev(_K_COLS), kmid(_K_COLS), knext(_K_COLS),
                     kprev(KV_WIDTH), kmid(KV_WIDTH), knext(KV_WIDTH)]
        args += [k, k, k, v, v, v]
    in_specs += [cspec(_K_COLS), cspec(KV_WIDTH)]
    args += [kc, vc]
    return pl.pallas_call(
        functools.partial(_attn_kernel, layer=l, steps=steps, band=band),
        grid=(B, steps),
        in_specs=in_specs,
        out_specs=pl.BlockSpec((None, rows, ATTN_WIDTH), lambda b, n: (b, n, 0)),
        out_shape=jax.ShapeDtypeStruct((B, T, ATTN_WIDTH), BF16),
        compiler_params=_params("arbitrary", "arbitrary"),
        name="attention_band" if band else "attention_ctx",
    )(*args)


def _pool_kernel(u_ref, w_ref, s_ref, o_ref, pad_ref, *, T, tc):
    zeros = jnp.zeros((POOL_HALO, POOL_WIDTH), F32)
    pad_ref[0:POOL_HALO, :] = zeros
    pad_ref[POOL_HALO + T:POOL_HALO + T + POOL_HALO, :] = zeros
    pad_ref[POOL_HALO:POOL_HALO + T, :] = u_ref[...]
    lane = lax.broadcasted_iota(jnp.int32, (tc, LANES), 1)
    upper = lane >= POOL_GROUP_DIM
    row = lax.broadcasted_iota(jnp.int32, (tc, LANES), 0)
    for c in range(T // tc):
        base = POOL_HALO + c * tc
        t = row + c * tc
        pooled = []
        for tile in range(POOL_WIDTH // LANES):
            w_small = 2 << (2 * tile)
            w_big = 2 * w_small
            cols = slice(tile * LANES, (tile + 1) * LANES)
            load = lambda d: pad_ref[base + d:base + d + tc, cols]
            inner = load(-(w_small // 2))
            for d in range(-(w_small // 2) + 1, w_small // 2):
                inner = inner + load(d)
            outer = load(-(w_big // 2))
            for d in list(range(-(w_big // 2) + 1, -(w_small // 2))) + list(range(w_small // 2, w_big // 2)):
                outer = outer + load(d)
            win = inner + jnp.where(upper, outer, 0.0)
            half = jnp.where(upper, w_big // 2, w_small // 2)
            lo = jnp.maximum(t - half, 0)
            hi = jnp.minimum(t + half, T)
            cnt = (hi - lo).astype(F32)
            pooled.append(win / cnt - load(0))
        p = jnp.concatenate(pooled, axis=1).astype(BF16)
        y = _dot(p, w_ref[...]) * s_ref[...]
        o_ref[c * tc:(c + 1) * tc, :] = y.astype(BF16)


def _pool(l, u, w_bd, scale):
    B, T, _ = u.shape
    tc = min(TOKEN_TILE, T)
    return pl.pallas_call(
        functools.partial(_pool_kernel, T=T, tc=tc),
        grid=(B,),
        in_specs=[
            pl.BlockSpec((None, T, POOL_WIDTH), lambda b: (b, 0, 0)),
            _layer(w_bd, l),
            _layer(scale, l),
        ],
        out_specs=pl.BlockSpec((None, T, POOL_WIDTH), lambda b: (b, 0, 0)),
        out_shape=jax.ShapeDtypeStruct((B, T, POOL_WIDTH), BF16),
        scratch_shapes=[pltpu.VMEM((T + 2 * POOL_HALO, POOL_WIDTH), F32)],
        compiler_params=_params("arbitrary"),
        name="pool",
    )(u, w_bd, scale)


def _fourier_factors(T):
    n1 = 1 << (int(math.log2(T)) // 2)
    return n1, T // n1


def _split_const(a):
    a = jnp.asarray(a, F32)
    hi = a.astype(BF16)
    return hi, (a - hi.astype(F32)).astype(BF16)


def _fourier_tables(T):
    n1, n2 = _fourier_factors(T)
    gd = FOURIER_GROUP_DIM
    t2 = np.arange(n2)[:, None, None]
    k1 = np.arange(n1)[None, :, None]
    t1 = np.arange(n1)[None, None, :]
    ph = -2.0 * np.pi * (t2 * k1 / T + t1 * k1 / n1)
    m1 = np.concatenate([np.cos(ph), np.sin(ph)], axis=1) / math.sqrt(n1)
    a3 = 2.0 * np.pi * np.outer(np.arange(n2), np.arange(n2)) / n2
    c3, s3 = np.cos(a3) / math.sqrt(n2), np.sin(a3) / math.sqrt(n2)
    m3 = np.block([[c3, s3], [-s3, c3]])
    c = np.arange(gd)
    ang = 2.0 * np.pi * np.outer(c, c) / gd
    eye = np.eye(FOURIER_WIDTH // gd)
    cd = np.concatenate([np.kron(eye, np.cos(ang)), np.kron(eye, np.sin(ang))], axis=0) / math.sqrt(gd)
    return _split_const(m1) + _split_const(m3) + _split_const(cd)


def _split(a):
    hi = a.astype(BF16)
    return hi, (a - hi.astype(F32)).astype(BF16)


def _dot3(a, b):
    return _dot(a[0], b[0]) + (_dot(a[1], b[0]) + _dot(a[0], b[1]))


def _fourier_kernel(f_ref, m1h_ref, m1l_ref, m3h_ref, m3l_ref, cdh_ref, cdl_ref, o_ref, a_ref, g_ref,
                    *, n1, n2, tc, unroll):
    T = n1 * n2
    wt = FOURIER_WIDTH // LANES
    lanes = lambda j: slice(j * LANES, (j + 1) * LANES)

    def over_t1(t2, carry):
        rows = pl.ds(t2, n1, stride=n2)
        x = jnp.concatenate([f_ref[j, rows, :] for j in range(wt)], axis=1)
        a = _dot3((m1h_ref[t2], m1l_ref[t2]), _split(x))
        off = pl.multiple_of(t2 * n1, n1)
        for j in range(wt):
            a_ref[j, pl.ds(off, n1), :] = a[:n1, lanes(j)]
            a_ref[wt + j, pl.ds(off, n1), :] = a[n1:, lanes(j)]
        return carry

    lax.fori_loop(0, n2, over_t1, 0, unroll=unroll)

    def over_t2(k1, carry):
        rows = pl.ds(k1, n2, stride=n1)
        b = jnp.concatenate(
            [jnp.concatenate([a_ref[h * wt + j, rows, :] for j in range(wt)], axis=1) for h in range(2)], axis=0)
        g = _dot3((m3h_ref[...], m3l_ref[...]), _split(b))
        for j in range(wt):
            g_ref[j, rows, :] = g[:n2, lanes(j)]
            g_ref[wt + j, rows, :] = g[n2:, lanes(j)]
        return carry

    lax.fori_loop(0, n1, over_t2, 0, unroll=unroll)

    for c in range(T // tc):
        r = slice(c * tc, (c + 1) * tc)
        g = jnp.concatenate([g_ref[j, r, :] for j in range(2 * wt)], axis=1)
        o_ref[r, :] = _dot3(_split(g), (cdh_ref[...], cdl_ref[...])).astype(BF16)


def _fourier(f, tables):
    B, wt, T, _ = f.shape
    n1, n2 = _fourier_factors(T)
    return pl.pallas_call(
        functools.partial(_fourier_kernel, n1=n1, n2=n2, tc=min(TOKEN_TILE, T), unroll=FOURIER_UNROLL),
        grid=(B,),
        in_specs=[pl.BlockSpec((None, wt, T, LANES), lambda b: (b, 0, 0, 0))] + [_resident(t.shape) for t in tables],
        out_specs=pl.BlockSpec((None, T, FOURIER_WIDTH), lambda b: (b, 0, 0)),
        out_shape=jax.ShapeDtypeStruct((B, T, FOURIER_WIDTH), BF16),
        scratch_shapes=[pltpu.VMEM((2 * wt, T, LANES), F32), pltpu.VMEM((2 * wt, T, LANES), F32)],
        compiler_params=_params("arbitrary"),
        name="fourier",
    )(f, *tables)


def _merge_kernel(x_ref, o_ref, p_ref, f_ref, g_ref, sh_ref, sc_ref, gt_ref,
                  wg_ref, wa_ref, wp_ref, wf_ref, wo_ref, out_ref, y_ref):
    x = x_ref[...]
    h = _modnorm(x, g_ref[...], sc_ref[...], sh_ref[...]).astype(BF16)
    o = o_ref[...]
    p = p_ref[...]
    f = f_ref[...]
    for j in range(D_MODEL // MERGE_CHUNK):
        c = slice(j * MERGE_CHUNK, (j + 1) * MERGE_CHUNK)
        gate = lambda b: jax.nn.sigmoid(_dot(h, wg_ref[:, b * D_MODEL + j * MERGE_CHUNK:
                                                        b * D_MODEL + (j + 1) * MERGE_CHUNK]))
        y = (gate(0) * _dot(o, wa_ref[:, c]) + gate(1) * _dot(p, wp_ref[:, c])
             + gate(2) * _dot(f, wf_ref[:, c]))
        y_ref[:, c] = y.astype(BF16)
    out_ref[...] = x + gt_ref[...] * _dot(y_ref[...], wo_ref[...])


def _merge(l, x, o, p, f, mod, gain, wg, wa, wp, wf, wo, ctx_row):
    B, T, _ = x.shape
    tm = min(MERGE_TILE, T)
    tok = lambda w: pl.BlockSpec((None, tm, w), lambda b, i: (b, i, 0))
    return pl.pallas_call(
        _merge_kernel,
        grid=(B, T // tm),
        in_specs=[
            tok(D_MODEL), tok(ATTN_WIDTH), tok(POOL_WIDTH), tok(FOURIER_WIDTH),
            _layer(gain, l),
            _mod_spec(l, 0, ctx_row), _mod_spec(l, 1, ctx_row), _mod_spec(l, 2, ctx_row),
            _layer(wg, l), _layer(wa, l), _layer(wp, l), _layer(wf, l), _layer(wo, l),
        ],
        out_specs=tok(D_MODEL),
        out_shape=jax.ShapeDtypeStruct((B, T, D_MODEL), F32),
        scratch_shapes=[pltpu.VMEM((tm, D_MODEL), BF16)],
        compiler_params=_params("arbitrary", "arbitrary"),
        name="merge",
    )(x, o, p, f, gain, mod, mod, mod, wg, wa, wp, wf, wo)


def _ffn_kernel(x_ref, xp_ref, xn_ref, g_ref, sh_ref, sc_ref, gt_ref, wu_ref, cw_ref, wd_ref, gf_ref,
                out_ref, h_ref, a_ref, *, tm, final):
    i = pl.program_id(1)
    last = pl.num_programs(1) - 1
    g, sh, sc = g_ref[...], sh_ref[...], sc_ref[...]
    x = x_ref[...]
    H = SUBLANES
    h_ref[H:H + tm, :] = _modnorm(x, g, sc, sh).astype(BF16)
    keep_prev = (i > 0).astype(F32)
    keep_next = (i < last).astype(F32)
    h_ref[0:H, :] = (_modnorm(xp_ref[...], g, sc, sh) * keep_prev).astype(BF16)
    h_ref[H + tm:H + tm + H, :] = (_modnorm(xn_ref[...], g, sc, sh) * keep_next).astype(BF16)
    hf = h_ref[...]
    rows = tm + 2 * H

    def conv(up, cols):
        w = cw_ref[:, cols]
        y = (pltpu.roll(up, 1, 0) * w[0:1] + up * w[1:2] + pltpu.roll(up, rows - 1, 0) * w[2:3])
        return y[H:H + tm]

    for j in range(D_FF // FF_CHUNK):
        cv = slice(j * FF_CHUNK, (j + 1) * FF_CHUNK)
        cg = slice(D_FF + j * FF_CHUNK, D_FF + (j + 1) * FF_CHUNK)
        val = conv(_dot(hf, wu_ref[:, cv]), cv)
        gate = conv(_dot(hf, wu_ref[:, cg]), cg)
        a_ref[:, cv] = (val * (gate * jax.nn.sigmoid(gate))).astype(BF16)
    y = x + gt_ref[...] * _dot(a_ref[...], wd_ref[...])
    if final:
        ms = jnp.mean(y * y, axis=-1, keepdims=True)
        y = y * lax.rsqrt(ms + EPS) * gf_ref[...]
    out_ref[...] = y


def _ffn(l, x, mod, gain, wu, cw, wd, gain_final, ctx_row, final):
    B, T, _ = x.shape
    tm = min(FFN_TILE, T)
    per = tm // SUBLANES
    nblk = T // SUBLANES
    tok = pl.BlockSpec((None, tm, D_MODEL), lambda b, i: (b, i, 0))
    prev = pl.BlockSpec((None, SUBLANES, D_MODEL), lambda b, i: (b, jnp.maximum(i * per - 1, 0), 0))
    nxt = pl.BlockSpec((None, SUBLANES, D_MODEL), lambda b, i: (b, jnp.minimum((i + 1) * per, nblk - 1), 0))
    return pl.pallas_call(
        functools.partial(_ffn_kernel, tm=tm, final=final),
        grid=(B, T // tm),
        in_specs=[
            tok, prev, nxt,
            _layer(gain, l),
            _mod_spec(l, 3, ctx_row), _mod_spec(l, 4, ctx_row), _mod_spec(l, 5, ctx_row),
            _layer(wu, l), _layer(cw, l), _layer(wd, l),
            _resident((1, D_MODEL)),
        ],
        out_specs=tok,
        out_shape=jax.ShapeDtypeStruct((B, T, D_MODEL), F32),
        scratch_shapes=[pltpu.VMEM((tm + 2 * SUBLANES, D_MODEL), BF16), pltpu.VMEM((tm, D_FF), BF16)],
        compiler_params=_params("arbitrary", "arbitrary"),
        name="ffn_final" if final else "ffn",
    )(x, x, x, gain, mod, mod, mod, wu, cw, wd, gain_final)


def _rope_tables(T):
    rows = T // GRID_W
    row = np.repeat(np.arange(rows), GRID_W).astype(np.float64)
    col = np.tile(np.arange(GRID_W), rows).astype(np.float64)
    n_freq = HEAD_DIM // 4
    inv_freq = ROPE_BASE ** (-np.arange(n_freq) / n_freq)
    ang = np.concatenate([row[:, None] * inv_freq[None, :], col[:, None] * inv_freq[None, :]], axis=1)
    cos, sin = np.cos(ang), np.sin(ang)
    return (jnp.asarray(np.tile(cos, (1, 4)), F32),
            jnp.asarray(np.concatenate([-sin, -sin, sin, sin], axis=1), F32))


def kernel(x, c, ctx, c_ctx, w_mod, b_mod, norm_mix, norm_ffn, w_in, attn_sink, pool_w, pool_scale,
           w_br_attn, w_br_pool, w_br_four, w_out, w_up, conv_w, w_down, norm_final):
    B, T, _ = x.shape
    L = ctx.shape[1]
    depth = w_mod.shape[0]
    assert B < COND_ROWS and T % max(MERGE_TILE, FFN_TILE) == 0 and T % (ATTN_Q_BLOCKS * BLOCK) == 0
    assert L % BLOCK == 0

    cond = jnp.concatenate([c, c_ctx[None], jnp.zeros((COND_ROWS - B - 1, D_MODEL), F32)], axis=0)
    mod = _adaln(cond, w_mod, b_mod).reshape(depth, COND_ROWS, N_MOD, 1, D_MODEL)

    cos_t, sin_t = _rope_tables(T)
    cos_c = jnp.ones((L, LANES), F32)
    sin_c = jnp.zeros((L, LANES), F32)
    four_lat = _fourier_tables(T)
    four_ctx = _fourier_tables(L)

    n_qkv = 2 * KV_WIDTH + ATTN_WIDTH
    n_tok = n_qkv + POOL_WIDTH + FOURIER_WIDTH
    qk_cols = jnp.asarray(_qk_columns(), jnp.int32)
    w_a = jnp.concatenate([jnp.take(w_in, qk_cols, axis=2), w_in[:, :, n_qkv:n_tok],
                           w_in[:, :, KV_WIDTH:2 * KV_WIDTH]], axis=2).astype(BF16)
    w_g = w_in[:, :, n_tok:].astype(BF16)
    w_a_br = (w_br_attn.reshape(depth, N_KV_HEADS, Q_PER_KV, HEAD_DIM, D_MODEL)
              .transpose(0, 2, 1, 3, 4).reshape(depth, ATTN_WIDTH, D_MODEL).astype(BF16))
    w_p_br = w_br_pool.astype(BF16)
    w_f_br = w_br_four.astype(BF16)
    w_o = w_out.astype(BF16)
    groups = pool_w.shape[1]
    w_pool = jnp.einsum("lgab,gh->lgahb", pool_w, jnp.eye(groups, dtype=pool_w.dtype)).reshape(
        depth, POOL_WIDTH, POOL_WIDTH).astype(BF16)
    p_scale = pool_scale.reshape(depth, 1, POOL_WIDTH)
    w_u = w_up.astype(BF16)
    w_d = w_down.astype(BF16)
    g_mix = norm_mix.reshape(depth, 1, D_MODEL)
    g_ffn = norm_ffn.reshape(depth, 1, D_MODEL)
    gain_final = norm_final.reshape(1, D_MODEL)

    xc = ctx
    for l in range(depth):
        last = l == depth - 1
        k, v, q, u, f = _inproj(l, x, mod, g_mix, w_a, cos_t, sin_t, None)
        kc, vc, qc, uc, fc = _inproj(l, xc, mod, g_mix, w_a, cos_c, sin_c, B)
        o = _attention(l, q, k, v, kc, vc, attn_sink, True)
        x = _merge(l, x, o, _pool(l, u, w_pool, p_scale), _fourier(f, four_lat), mod, g_mix,
                   w_g, w_a_br, w_p_br, w_f_br, w_o, None)
        if not last:
            oc = _attention(l, qc, None, None, kc, vc, attn_sink, False)
            xc = _merge(l, xc, oc, _pool(l, uc, w_pool, p_scale), _fourier(fc, four_ctx), mod, g_mix,
                        w_g, w_a_br, w_p_br, w_f_br, w_o, B)
        x = _ffn(l, x, mod, g_ffn, w_u, conv_w, w_d, gain_final, None, last)
        if not last:
            xc = _ffn(l, xc, mod, g_ffn, w_u, conv_w, w_d, gain_final, B, False)
    return x
```

```python
import functools
import math

import numpy as np
import jax
import jax.numpy as jnp
from jax import lax
from jax.experimental import pallas as pl
from jax.experimental.pallas import tpu as pltpu

D_MODEL = 1024
N_Q_HEADS = 8
N_KV_HEADS = 2
HEAD_DIM = 64
Q_PER_KV = N_Q_HEADS // N_KV_HEADS
ATTN_WIDTH = N_Q_HEADS * HEAD_DIM
KV_WIDTH = N_KV_HEADS * HEAD_DIM
BLOCK = 128
GRID_W = 64
ROPE_BASE = 10000.0
POOL_WIDTH = 256
POOL_GROUP_DIM = 64
FOURIER_WIDTH = 256
FOURIER_GROUP_DIM = 64
D_FF = 2816
N_MOD = 6
EPS = 1e-6
NEG_INF = -1e30
LOG2E = math.log2(math.e)

LANES = 128
SUBLANES = 8
VMEM_LIMIT_BYTES = 56 * 1024 * 1024

TOKEN_TILE = 512
MERGE_TILE = 1024
FFN_TILE = 1024
ATTN_Q_BLOCKS = 2
FF_CHUNK = 256
MERGE_CHUNK = 256
FOURIER_UNROLL = 8
POOL_HALO = 8
ADALN_TILE = 1536
COND_ROWS = 8

F32 = jnp.float32
BF16 = jnp.bfloat16


def _dot(a, b):
    return jnp.dot(a, b, preferred_element_type=F32)


def _dot_nt(a, b):
    return lax.dot_general(a, b, (((1,), (1,)), ((), ())), preferred_element_type=F32)


def _modnorm(x, g, sc, sh):
    ms = jnp.mean(x * x, axis=-1, keepdims=True)
    y = x * lax.rsqrt(ms + EPS) * g
    return y * (1.0 + sc) + sh


def _resident(shape):
    nd = len(shape)
    return pl.BlockSpec(shape, lambda *_: (0,) * nd, pipeline_mode=pl.Buffered(1))


def _layer(arr, l):
    nd = arr.ndim - 1
    return pl.BlockSpec((None,) + arr.shape[1:], lambda *_: (l,) + (0,) * nd, pipeline_mode=pl.Buffered(1))


def _mod_spec(l, j, ctx_row):
    if ctx_row is None:
        return pl.BlockSpec((None, None, None, 1, D_MODEL), lambda b, i: (l, b, j, 0, 0))
    return pl.BlockSpec((None, None, None, 1, D_MODEL), lambda b, i: (l, ctx_row, j, 0, 0))


def _params(*sem):
    return pltpu.CompilerParams(dimension_semantics=sem, vmem_limit_bytes=VMEM_LIMIT_BYTES)


def _adaln_kernel(c_ref, w_ref, b_ref, o_ref):
    c = c_ref[...]
    a = (c * jax.nn.sigmoid(c)).astype(BF16)
    o_ref[...] = _dot(a, w_ref[...].astype(BF16)) + b_ref[...]


def _adaln(cond, w_mod, b_mod):
    depth, _, width = w_mod.shape
    return pl.pallas_call(
        _adaln_kernel,
        grid=(depth, width // ADALN_TILE),
        in_specs=[
            pl.BlockSpec((COND_ROWS, D_MODEL), lambda l, j: (0, 0)),
            pl.BlockSpec((None, D_MODEL, ADALN_TILE), lambda l, j: (l, 0, j)),
            pl.BlockSpec((None, 1, ADALN_TILE), lambda l, j: (l, 0, j)),
        ],
        out_specs=pl.BlockSpec((None, COND_ROWS, ADALN_TILE), lambda l, j: (l, 0, j)),
        out_shape=jax.ShapeDtypeStruct((depth, COND_ROWS, width), F32),
        compiler_params=_params("arbitrary", "arbitrary"),
        name="adaln",
    )(cond, w_mod, b_mod.reshape(depth, 1, width))


_K_COLS = 2 * KV_WIDTH
IN_TOKEN_COLS = 2 * KV_WIDTH + ATTN_WIDTH + POOL_WIDTH + FOURIER_WIDTH
_A_SPLITS = (0, KV_WIDTH, 2 * KV_WIDTH, 2 * KV_WIDTH + ATTN_WIDTH, 2 * KV_WIDTH + ATTN_WIDTH + POOL_WIDTH,
             IN_TOKEN_COLS)


def _inproj_kernel(x_ref, g_ref, sh_ref, sc_ref, w_ref, cos_ref, sin_ref,
                   k_ref, v_ref, q_ref, u_ref, f_ref):
    h = _modnorm(x_ref[...], g_ref[...], sc_ref[...], sh_ref[...]).astype(BF16)
    cos = cos_ref[...]
    sin = sin_ref[...]

    lane = lax.broadcasted_iota(jnp.int32, cos.shape, 1)
    first = (lane % (HEAD_DIM // 2)) < HEAD_DIM // 4
    low = lane < HEAD_DIM

    def rope(t):
        partner = jnp.where(first, pltpu.roll(t, LANES - HEAD_DIM // 4, 1), pltpu.roll(t, HEAD_DIM // 4, 1))
        return t * cos + partner * sin

    k0, v0, q0, u0, f0, end = _A_SPLITS
    tile = lambda a, j: a[:, j * LANES:(j + 1) * LANES]
    kvq = _dot(h, w_ref[:, k0:u0])
    k = rope(tile(kvq, 0))
    k_swapped = pltpu.roll(k, HEAD_DIM, 1)
    k_ref[:, 0:LANES] = jnp.where(low, k, k_swapped).astype(BF16)
    k_ref[:, LANES:2 * LANES] = jnp.where(low, k_swapped, k).astype(BF16)
    v_ref[...] = tile(kvq, v0 // LANES).astype(BF16)
    scale = HEAD_DIM ** -0.5 * LOG2E
    for j in range(ATTN_WIDTH // LANES):
        q_ref[:, j * LANES:(j + 1) * LANES] = (rope(tile(kvq, q0 // LANES + j)) * scale).astype(BF16)
    uf = _dot(h, w_ref[:, u0:end])
    u_ref[...] = uf[:, 0:f0 - u0]
    for j in range(FOURIER_WIDTH // LANES):
        f_ref[j] = tile(uf, (f0 - u0) // LANES + j)


def _inproj(l, x, mod, gain, w_a, cos_t, sin_t, ctx_row):
    B, T, _ = x.shape
    tm = min(TOKEN_TILE, T)
    tok = lambda w: pl.BlockSpec((None, tm, w), lambda b, i: (b, i, 0))
    return pl.pallas_call(
        _inproj_kernel,
        grid=(B, T // tm),
        in_specs=[
            tok(D_MODEL),
            _layer(gain, l),
            _mod_spec(l, 0, ctx_row), _mod_spec(l, 1, ctx_row),
            pl.BlockSpec((None, D_MODEL, IN_TOKEN_COLS), lambda b, i: (l, 0, 0), pipeline_mode=pl.Buffered(1)),
            pl.BlockSpec((tm, LANES), lambda b, i: (i, 0)),
            pl.BlockSpec((tm, LANES), lambda b, i: (i, 0)),
        ],
        out_specs=[tok(_K_COLS), tok(KV_WIDTH), tok(ATTN_WIDTH), tok(POOL_WIDTH),
                   pl.BlockSpec((None, FOURIER_WIDTH // LANES, tm, LANES), lambda b, i: (b, 0, i, 0))],
        out_shape=[
            jax.ShapeDtypeStruct((B, T, _K_COLS), BF16),
            jax.ShapeDtypeStruct((B, T, KV_WIDTH), BF16),
            jax.ShapeDtypeStruct((B, T, ATTN_WIDTH), BF16),
            jax.ShapeDtypeStruct((B, T, POOL_WIDTH), F32),
            jax.ShapeDtypeStruct((B, FOURIER_WIDTH // LANES, T, LANES), F32),
        ],
        compiler_params=_params("arbitrary", "arbitrary"),
        name="inproj",
    )(x, gain, mod, mod, w_a, cos_t, sin_t)


def _attn_kernel(sink_ref, q_ref, *refs, layer, steps, band):
    if band:
        kp_ref, km_ref, kn_ref, vp_ref, vm_ref, vn_ref, kc_ref, vc_ref, o_ref = refs
    else:
        kc_ref, vc_ref, o_ref = refs
    n = pl.program_id(1)
    rows = Q_PER_KV * BLOCK
    lane = lax.broadcasted_iota(jnp.int32, (BLOCK, LANES), 1)
    low = lane < HEAD_DIM
    rgrp = lax.broadcasted_iota(jnp.int32, (rows, 1), 0) // BLOCK
    if band:
        qi = lax.broadcasted_iota(jnp.int32, (rows, BLOCK), 0) % BLOCK
        kj = lax.broadcasted_iota(jnp.int32, (rows, BLOCK), 1)
        in_prev = kj >= qi
        in_next = kj <= qi
    blk = lambda ref, j, cols: ref[j * BLOCK:(j + 1) * BLOCK, cols]
    every = slice(None)

    def one_head(q0, h, k_parts, v_all, ok_prev, ok_next):
        qs = []
        for t in range(Q_PER_KV // 2):
            c0 = (h * (Q_PER_KV // 2) + t) * LANES
            qt = q_ref[q0:q0 + BLOCK, c0:c0 + LANES]
            qs.append(jnp.where(low, qt, jnp.zeros_like(qt)))
            qs.append(jnp.where(low, jnp.zeros_like(qt), qt))
        qh = jnp.concatenate(qs, axis=0)
        s = _dot_nt(qh, jnp.concatenate(k_parts, axis=0))
        tiles = [s[:, j * LANES:(j + 1) * LANES] for j in range(s.shape[1] // LANES)]
        if band:
            tiles[0] = jnp.where(ok_prev, tiles[0], NEG_INF)
            tiles[2] = jnp.where(ok_next, tiles[2], NEG_INF)
        sink = jnp.zeros((rows, 1), F32)
        for g in range(Q_PER_KV):
            sink = jnp.where(rgrp == g, sink_ref[layer, h * Q_PER_KV + g] * LOG2E, sink)
        m_t = tiles[0]
        for t in tiles[1:]:
            m_t = jnp.maximum(m_t, t)
        m = jnp.maximum(jnp.max(m_t, axis=-1, keepdims=True), sink)
        p = jnp.concatenate([jnp.exp2(t - m).astype(BF16) for t in tiles], axis=1)
        return _dot(p, v_all), jnp.exp2(sink - m)

    n_sub = ATTN_Q_BLOCKS if band else 1
    for sub in range(n_sub):
        if band:
            pick = lambda pm, mid, nx, j, cols: (blk(pm, 0, cols) if j < 0 else
                                                 blk(nx, 0, cols) if j >= n_sub else blk(mid, j, cols))
            ok_prev = jnp.logical_and(in_prev, n > 0) if sub == 0 else in_prev
            ok_next = jnp.logical_and(in_next, n < steps - 1) if sub == n_sub - 1 else in_next
            v_all = jnp.concatenate([pick(vp_ref, vm_ref, vn_ref, sub + d, every) for d in (-1, 0, 1)]
                                    + [vc_ref[...]], axis=0)
        else:
            ok_prev = ok_next = None
            v_all = vc_ref[...]
        v_low = lax.broadcasted_iota(jnp.int32, v_all.shape, 1) < HEAD_DIM
        ones = jnp.ones_like(v_all)
        v_aug = [jnp.where(v_low, v_all, ones), jnp.where(v_low, ones, v_all)]
        outs = []
        for h in range(N_KV_HEADS):
            ks = slice(h * LANES, (h + 1) * LANES)
            k_parts = [kc_ref[:, ks]]
            if band:
                k_parts = [pick(kp_ref, km_ref, kn_ref, sub + d, ks) for d in (-1, 0, 1)] + k_parts
            outs.append(one_head(sub * BLOCK, h, k_parts, v_aug[h], ok_prev, ok_next))
        (o0, e0), (o1, e1) = outs
        for g in range(Q_PER_KV):
            r = slice(g * BLOCK, (g + 1) * BLOCK)
            num = jnp.where(low, o0[r], o1[r])
            den = pltpu.roll(jnp.where(low, o1[r], o0[r]), HEAD_DIM, 1) + jnp.where(low, e0[r], e1[r])
            o_ref[sub * BLOCK:(sub + 1) * BLOCK, g * LANES:(g + 1) * LANES] = (num / den).astype(BF16)


def _attention(l, q, k, v, kc, vc, sink, band):
    B, T, _ = q.shape
    L = kc.shape[1]
    nq = ATTN_Q_BLOCKS if band else 1
    rows = nq * BLOCK
    steps = T // rows
    nb = T // BLOCK
    qspec = pl.BlockSpec((None, rows, ATTN_WIDTH), lambda b, n: (b, n, 0))
    kprev = lambda w: pl.BlockSpec((None, BLOCK, w), lambda b, n: (b, jnp.maximum(n * nq - 1, 0), 0))
    kmid = lambda w: pl.BlockSpec((None, rows, w), lambda b, n: (b, n, 0))
    knext = lambda w: pl.BlockSpec((None, BLOCK, w), lambda b, n: (b, jnp.minimum((n + 1) * nq, nb - 1), 0))
    cspec = lambda w: pl.BlockSpec((None, L, w), lambda b, n: (b, 0, 0))
    in_specs = [pl.BlockSpec(memory_space=pltpu.SMEM), qspec]
    args = [sink, q]
    if band:
        in_specs += [kprev(_K_COLS), kmid(_K_COLS), knext(_K_COLS),
                     kprev(KV_WIDTH), kmid(KV_WIDTH), knext(KV_WIDTH)]
        args += [k, k, k, v, v, v]
    in_specs += [cspec(_K_COLS), cspec(KV_WIDTH)]
    args += [kc, vc]
    return pl.pallas_call(
        functools.partial(_attn_kernel, layer=l, steps=steps, band=band),
        grid=(B, steps),
        in_specs=in_specs,
        out_specs=pl.BlockSpec((None, rows, ATTN_WIDTH), lambda b, n: (b, n, 0)),
        out_shape=jax.ShapeDtypeStruct((B, T, ATTN_WIDTH), BF16),
        compiler_params=_params("arbitrary", "arbitrary"),
        name="attention_band" if band else "attention_ctx",
    )(*args)


def _pool_kernel(u_ref, w_ref, s_ref, o_ref, pad_ref, *, T, tc):
    zeros = jnp.zeros((POOL_HALO, POOL_WIDTH), F32)
    pad_ref[0:POOL_HALO, :] = zeros
    pad_ref[POOL_HALO + T:POOL_HALO + T + POOL_HALO, :] = zeros
    pad_ref[POOL_HALO:POOL_HALO + T, :] = u_ref[...]
    lane = lax.broadcasted_iota(jnp.int32, (tc, LANES), 1)
    upper = lane >= POOL_GROUP_DIM
    row = lax.broadcasted_iota(jnp.int32, (tc, LANES), 0)
    for c in range(T // tc):
        base = POOL_HALO + c * tc
        t = row + c * tc
        pooled = []
        for tile in range(POOL_WIDTH // LANES):
            w_small = 2 << (2 * tile)
            w_big = 2 * w_small
            cols = slice(tile * LANES, (tile + 1) * LANES)
            load = lambda d: pad_ref[base + d:base + d + tc, cols]
            inner = load(-(w_small // 2))
            for d in range(-(w_small // 2) + 1, w_small // 2):
                inner = inner + load(d)
            outer = load(-(w_big // 2))
            for d in list(range(-(w_big // 2) + 1, -(w_small // 2))) + list(range(w_small // 2, w_big // 2)):
                outer = outer + load(d)
            win = inner + jnp.where(upper, outer, 0.0)
            half = jnp.where(upper, w_big // 2, w_small // 2)
            lo = jnp.maximum(t - half, 0)
            hi = jnp.minimum(t + half, T)
            cnt = (hi - lo).astype(F32)
            pooled.append(win / cnt - load(0))
        p = jnp.concatenate(pooled, axis=1).astype(BF16)
        y = _dot(p, w_ref[...]) * s_ref[...]
        o_ref[c * tc:(c + 1) * tc, :] = y.astype(BF16)


def _pool(l, u, w_bd, scale):
    B, T, _ = u.shape
    tc = min(TOKEN_TILE, T)
    return pl.pallas_call(
        functools.partial(_pool_kernel, T=T, tc=tc),
        grid=(B,),
        in_specs=[
            pl.BlockSpec((None, T, POOL_WIDTH), lambda b: (b, 0, 0)),
            _layer(w_bd, l),
            _layer(scale, l),
        ],
        out_specs=pl.BlockSpec((None, T, POOL_WIDTH), lambda b: (b, 0, 0)),
        out_shape=jax.ShapeDtypeStruct((B, T, POOL_WIDTH), BF16),
        scratch_shapes=[pltpu.VMEM((T + 2 * POOL_HALO, POOL_WIDTH), F32)],
        compiler_params=_params("arbitrary"),
        name="pool",
    )(u, w_bd, scale)


def _fourier_factors(T):
    n1 = 1 << (int(math.log2(T)) // 2)
    return n1, T // n1


def _split_const(a):
    a = jnp.asarray(a, F32)
    hi = a.astype(BF16)
    return hi, (a - hi.astype(F32)).astype(BF16)


def _fourier_tables(T):
    n1, n2 = _fourier_factors(T)
    gd = FOURIER_GROUP_DIM
    t2 = np.arange(n2)[:, None, None]
    k1 = np.arange(n1)[None, :, None]
    t1 = np.arange(n1)[None, None, :]
    ph = -2.0 * np.pi * (t2 * k1 / T + t1 * k1 / n1)
    m1 = np.concatenate([np.cos(ph), np.sin(ph)], axis=1) / math.sqrt(n1)
    a3 = 2.0 * np.pi * np.outer(np.arange(n2), np.arange(n2)) / n2
    c3, s3 = np.cos(a3) / math.sqrt(n2), np.sin(a3) / math.sqrt(n2)
    m3 = np.block([[c3, s3], [-s3, c3]])
    c = np.arange(gd)
    ang = 2.0 * np.pi * np.outer(c, c) / gd
    eye = np.eye(FOURIER_WIDTH // gd)
    cd = np.concatenate([np.kron(eye, np.cos(ang)), np.kron(eye, np.sin(ang))], axis=0) / math.sqrt(gd)
    return _split_const(m1) + _split_const(m3) + _split_const(cd)


def _split(a):
    hi = a.astype(BF16)
    return hi, (a - hi.astype(F32)).astype(BF16)


def _dot3(a, b):
    return _dot(a[0], b[0]) + (_dot(a[1], b[0]) + _dot(a[0], b[1]))


def _fourier_kernel(f_ref, m1h_ref, m1l_ref, m3h_ref, m3l_ref, cdh_ref, cdl_ref, o_ref, a_ref, g_ref,
                    *, n1, n2, tc, unroll):
    T = n1 * n2
    wt = FOURIER_WIDTH // LANES
    lanes = lambda j: slice(j * LANES, (j + 1) * LANES)

    def over_t1(t2, carry):
        rows = pl.ds(t2, n1, stride=n2)
        x = jnp.concatenate([f_ref[j, rows, :] for j in range(wt)], axis=1)
        a = _dot3((m1h_ref[t2], m1l_ref[t2]), _split(x))
        off = pl.multiple_of(t2 * n1, n1)
        for j in range(wt):
            a_ref[j, pl.ds(off, n1), :] = a[:n1, lanes(j)]
            a_ref[wt + j, pl.ds(off, n1), :] = a[n1:, lanes(j)]
        return carry

    lax.fori_loop(0, n2, over_t1, 0, unroll=unroll)

    def over_t2(k1, carry):
        rows = pl.ds(k1, n2, stride=n1)
        b = jnp.concatenate(
            [jnp.concatenate([a_ref[h * wt + j, rows, :] for j in range(wt)], axis=1) for h in range(2)], axis=0)
        g = _dot3((m3h_ref[...], m3l_ref[...]), _split(b))
        for j in range(wt):
            g_ref[j, rows, :] = g[:n2, lanes(j)]
            g_ref[wt + j, rows, :] = g[n2:, lanes(j)]
        return carry

    lax.fori_loop(0, n1, over_t2, 0, unroll=unroll)

    for c in range(T // tc):
        r = slice(c * tc, (c + 1) * tc)
        g = jnp.concatenate([g_ref[j, r, :] for j in range(2 * wt)], axis=1)
        o_ref[r, :] = _dot3(_split(g), (cdh_ref[...], cdl_ref[...])).astype(BF16)


def _fourier(f, tables):
    B, wt, T, _ = f.shape
    n1, n2 = _fourier_factors(T)
    return pl.pallas_call(
        functools.partial(_fourier_kernel, n1=n1, n2=n2, tc=min(TOKEN_TILE, T), unroll=FOURIER_UNROLL),
        grid=(B,),
        in_specs=[pl.BlockSpec((None, wt, T, LANES), lambda b: (b, 0, 0, 0))] + [_resident(t.shape) for t in tables],
        out_specs=pl.BlockSpec((None, T, FOURIER_WIDTH), lambda b: (b, 0, 0)),
        out_shape=jax.ShapeDtypeStruct((B, T, FOURIER_WIDTH), BF16),
        scratch_shapes=[pltpu.VMEM((2 * wt, T, LANES), F32), pltpu.VMEM((2 * wt, T, LANES), F32)],
        compiler_params=_params("arbitrary"),
        name="fourier",
    )(f, *tables)


def _merge_kernel(x_ref, o_ref, p_ref, f_ref, g_ref, sh_ref, sc_ref, gt_ref,
                  wg_ref, wa_ref, wp_ref, wf_ref, wo_ref, out_ref, y_ref):
    x = x_ref[...]
    h = _modnorm(x, g_ref[...], sc_ref[...], sh_ref[...]).astype(BF16)
    o = o_ref[...]
    p = p_ref[...]
    f = f_ref[...]
    for j in range(D_MODEL // MERGE_CHUNK):
        c = slice(j * MERGE_CHUNK, (j + 1) * MERGE_CHUNK)
        gate = lambda b: jax.nn.sigmoid(_dot(h, wg_ref[:, b * D_MODEL + j * MERGE_CHUNK:
                                                        b * D_MODEL + (j + 1) * MERGE_CHUNK]))
        y = (gate(0) * _dot(o, wa_ref[:, c]) + gate(1) * _dot(p, wp_ref[:, c])
             + gate(2) * _dot(f, wf_ref[:, c]))
        y_ref[:, c] = y.astype(BF16)
    out_ref[...] = x + gt_ref[...] * _dot(y_ref[...], wo_ref[...])


def _merge(l, x, o, p, f, mod, gain, wg, wa, wp, wf, wo, ctx_row):
    B, T, _ = x.shape
    tm = min(MERGE_TILE, T)
    tok = lambda w: pl.BlockSpec((None, tm, w), lambda b, i: (b, i, 0))
    return pl.pallas_call(
        _merge_kernel,
        grid=(B, T // tm),
        in_specs=[
            tok(D_MODEL), tok(ATTN_WIDTH), tok(POOL_WIDTH), tok(FOURIER_WIDTH),
            _layer(gain, l),
            _mod_spec(l, 0, ctx_row), _mod_spec(l, 1, ctx_row), _mod_spec(l, 2, ctx_row),
            _layer(wg, l), _layer(wa, l), _layer(wp, l), _layer(wf, l), _layer(wo, l),
        ],
        out_specs=tok(D_MODEL),
        out_shape=jax.ShapeDtypeStruct((B, T, D_MODEL), F32),
        scratch_shapes=[pltpu.VMEM((tm, D_MODEL), BF16)],
        compiler_params=_params("arbitrary", "arbitrary"),
        name="merge",
    )(x, o, p, f, gain, mod, mod, mod, wg, wa, wp, wf, wo)


def _ffn_kernel(x_ref, xp_ref, xn_ref, g_ref, sh_ref, sc_ref, gt_ref, wu_ref, cw_ref, wd_ref, gf_ref,
                out_ref, h_ref, a_ref, *, tm, final):
    i = pl.program_id(1)
    last = pl.num_programs(1) - 1
    g, sh, sc = g_ref[...], sh_ref[...], sc_ref[...]
    x = x_ref[...]
    H = SUBLANES
    h_ref[H:H + tm, :] = _modnorm(x, g, sc, sh).astype(BF16)
    keep_prev = (i > 0).astype(F32)
    keep_next = (i < last).astype(F32)
    h_ref[0:H, :] = (_modnorm(xp_ref[...], g, sc, sh) * keep_prev).astype(BF16)
    h_ref[H + tm:H + tm + H, :] = (_modnorm(xn_ref[...], g, sc, sh) * keep_next).astype(BF16)
    hf = h_ref[...]
    rows = tm + 2 * H

    def conv(up, cols):
        w = cw_ref[:, cols]
        y = (pltpu.roll(up, 1, 0) * w[0:1] + up * w[1:2] + pltpu.roll(up, rows - 1, 0) * w[2:3])
        return y[H:H + tm]

    for j in range(D_FF // FF_CHUNK):
        cv = slice(j * FF_CHUNK, (j + 1) * FF_CHUNK)
        cg = slice(D_FF + j * FF_CHUNK, D_FF + (j + 1) * FF_CHUNK)
        val = conv(_dot(hf, wu_ref[:, cv]), cv)
        gate = conv(_dot(hf, wu_ref[:, cg]), cg)
        a_ref[:, cv] = (val * (gate * jax.nn.sigmoid(gate))).astype(BF16)
    y = x + gt_ref[...] * _dot(a_ref[...], wd_ref[...])
    if final:
        ms = jnp.mean(y * y, axis=-1, keepdims=True)
        y = y * lax.rsqrt(ms + EPS) * gf_ref[...]
    out_ref[...] = y


def _ffn(l, x, mod, gain, wu, cw, wd, gain_final, ctx_row, final):
    B, T, _ = x.shape
    tm = min(FFN_TILE, T)
    per = tm // SUBLANES
    nblk = T // SUBLANES
    tok = pl.BlockSpec((None, tm, D_MODEL), lambda b, i: (b, i, 0))
    prev = pl.BlockSpec((None, SUBLANES, D_MODEL), lambda b, i: (b, jnp.maximum(i * per - 1, 0), 0))
    nxt = pl.BlockSpec((None, SUBLANES, D_MODEL), lambda b, i: (b, jnp.minimum((i + 1) * per, nblk - 1), 0))
    return pl.pallas_call(
        functools.partial(_ffn_kernel, tm=tm, final=final),
        grid=(B, T // tm),
        in_specs=[
            tok, prev, nxt,
            _layer(gain, l),
            _mod_spec(l, 3, ctx_row), _mod_spec(l, 4, ctx_row), _mod_spec(l, 5, ctx_row),
            _layer(wu, l), _layer(cw, l), _layer(wd, l),
            _resident((1, D_MODEL)),
        ],
        out_specs=tok,
        out_shape=jax.ShapeDtypeStruct((B, T, D_MODEL), F32),
        scratch_shapes=[pltpu.VMEM((tm + 2 * SUBLANES, D_MODEL), BF16), pltpu.VMEM((tm, D_FF), BF16)],
        compiler_params=_params("arbitrary", "arbitrary"),
        name="ffn_final" if final else "ffn",
    )(x, x, x, gain, mod, mod, mod, wu, cw, wd, gain_final)


def _rope_tables(T):
    rows = T // GRID_W
    row = np.repeat(np.arange(rows), GRID_W).astype(np.float64)
    col = np.tile(np.arange(GRID_W), rows).astype(np.float64)
    n_freq = HEAD_DIM // 4
    inv_freq = ROPE_BASE ** (-np.arange(n_freq) / n_freq)
    ar, ac = row[:, None] * inv_freq[None, :], col[:, None] * inv_freq[None, :]
    cos = np.concatenate([np.cos(ar), np.cos(ar), np.cos(ac), np.cos(ac)], axis=1)
    sin = np.concatenate([-np.sin(ar), np.sin(ar), -np.sin(ac), np.sin(ac)], axis=1)
    reps = LANES // HEAD_DIM
    return jnp.asarray(np.tile(cos, (1, reps)), F32), jnp.asarray(np.tile(sin, (1, reps)), F32)


def kernel(x, c, ctx, c_ctx, w_mod, b_mod, norm_mix, norm_ffn, w_in, attn_sink, pool_w, pool_scale,
           w_br_attn, w_br_pool, w_br_four, w_out, w_up, conv_w, w_down, norm_final):
    B, T, _ = x.shape
    L = ctx.shape[1]
    depth = w_mod.shape[0]
    assert B < COND_ROWS and T % max(MERGE_TILE, FFN_TILE) == 0 and T % (ATTN_Q_BLOCKS * BLOCK) == 0
    assert L % BLOCK == 0

    cond = jnp.concatenate([c, c_ctx[None], jnp.zeros((COND_ROWS - B - 1, D_MODEL), F32)], axis=0)
    mod = _adaln(cond, w_mod, b_mod).reshape(depth, COND_ROWS, N_MOD, 1, D_MODEL)

    cos_t, sin_t = _rope_tables(T)
    cos_c = jnp.ones((L, LANES), F32)
    sin_c = jnp.zeros((L, LANES), F32)
    four_lat = _fourier_tables(T)
    four_ctx = _fourier_tables(L)

    w_a = w_in.astype(BF16)
    w_g = w_in[:, :, IN_TOKEN_COLS:].astype(BF16)
    w_a_br = (w_br_attn.reshape(depth, N_KV_HEADS, Q_PER_KV, HEAD_DIM, D_MODEL)
              .transpose(0, 2, 1, 3, 4).reshape(depth, ATTN_WIDTH, D_MODEL).astype(BF16))
    w_p_br = w_br_pool.astype(BF16)
    w_f_br = w_br_four.astype(BF16)
    w_o = w_out.astype(BF16)
    groups = pool_w.shape[1]
    w_pool = jnp.einsum("lgab,gh->lgahb", pool_w, jnp.eye(groups, dtype=pool_w.dtype)).reshape(
        depth, POOL_WIDTH, POOL_WIDTH).astype(BF16)
    p_scale = pool_scale.reshape(depth, 1, POOL_WIDTH)
    w_u = w_up.astype(BF16)
    w_d = w_down.astype(BF16)
    g_mix = norm_mix.reshape(depth, 1, D_MODEL)
    g_ffn = norm_ffn.reshape(depth, 1, D_MODEL)
    gain_final = norm_final.reshape(1, D_MODEL)

    xc = ctx
    for l in range(depth):
        last = l == depth - 1
        k, v, q, u, f = _inproj(l, x, mod, g_mix, w_a, cos_t, sin_t, None)
        kc, vc, qc, uc, fc = _inproj(l, xc, mod, g_mix, w_a, cos_c, sin_c, B)
        o = _attention(l, q, k, v, kc, vc, attn_sink, True)
        x = _merge(l, x, o, _pool(l, u, w_pool, p_scale), _fourier(f, four_lat), mod, g_mix,
                   w_g, w_a_br, w_p_br, w_f_br, w_o, None)
        if not last:
            oc = _attention(l, qc, None, None, kc, vc, attn_sink, False)
            xc = _merge(l, xc, oc, _pool(l, uc, w_pool, p_scale), _fourier(fc, four_ctx), mod, g_mix,
                        w_g, w_a_br, w_p_br, w_f_br, w_o, B)
        x = _ffn(l, x, mod, g_ffn, w_u, conv_w, w_d, gain_final, None, last)
        if not last:
            xc = _ffn(l, xc, mod, g_ffn, w_u, conv_w, w_d, gain_final, B, False)
    return x
```

```python
import functools
import math

import numpy as np
import jax
import jax.numpy as jnp
from jax import lax
from jax.experimental import pallas as pl
from jax.experimental.pallas import tpu as pltpu

D_MODEL = 1024
N_Q_HEADS = 8
N_KV_HEADS = 2
HEAD_DIM = 64
Q_PER_KV = N_Q_HEADS // N_KV_HEADS
ATTN_WIDTH = N_Q_HEADS * HEAD_DIM
KV_WIDTH = N_KV_HEADS * HEAD_DIM
BLOCK = 128
GRID_W = 64
ROPE_BASE = 10000.0
POOL_WIDTH = 256
POOL_GROUP_DIM = 64
FOURIER_WIDTH = 256
FOURIER_GROUP_DIM = 64
D_FF = 2816
N_MOD = 6
EPS = 1e-6
NEG_INF = -1e30
LOG2E = math.log2(math.e)

LANES = 128
SUBLANES = 8
VMEM_LIMIT_BYTES = 56 * 1024 * 1024

TOKEN_TILE = 512
MERGE_TILE = 1024
FFN_TILE = 1024
ATTN_Q_BLOCKS = 8
FF_CHUNK = 256
MERGE_CHUNK = 256
FOURIER_UNROLL = 8
POOL_HALO = 8
ADALN_TILE = 1536
COND_ROWS = 8

F32 = jnp.float32
BF16 = jnp.bfloat16


def _dot(a, b):
    return jnp.dot(a, b, preferred_element_type=F32)


def _dot_nt(a, b):
    return lax.dot_general(a, b, (((1,), (1,)), ((), ())), preferred_element_type=F32)


def _modnorm(x, g, sc, sh):
    ms = jnp.mean(x * x, axis=-1, keepdims=True)
    y = x * lax.rsqrt(ms + EPS) * g
    return y * (1.0 + sc) + sh


def _resident(shape):
    nd = len(shape)
    return pl.BlockSpec(shape, lambda *_: (0,) * nd, pipeline_mode=pl.Buffered(1))


def _layer(arr, l):
    nd = arr.ndim - 1
    return pl.BlockSpec((None,) + arr.shape[1:], lambda *_: (l,) + (0,) * nd, pipeline_mode=pl.Buffered(1))


def _mod_spec(l, j, ctx_row):
    if ctx_row is None:
        return pl.BlockSpec((None, None, None, 1, D_MODEL), lambda b, i: (l, b, j, 0, 0))
    return pl.BlockSpec((None, None, None, 1, D_MODEL), lambda b, i: (l, ctx_row, j, 0, 0))


def _params(*sem):
    return pltpu.CompilerParams(dimension_semantics=sem, vmem_limit_bytes=VMEM_LIMIT_BYTES)


def _adaln_kernel(c_ref, w_ref, b_ref, o_ref):
    c = c_ref[...]
    a = (c * jax.nn.sigmoid(c)).astype(BF16)
    o_ref[...] = _dot(a, w_ref[...].astype(BF16)) + b_ref[...]


def _adaln(cond, w_mod, b_mod):
    depth, _, width = w_mod.shape
    return pl.pallas_call(
        _adaln_kernel,
        grid=(depth, width // ADALN_TILE),
        in_specs=[
            pl.BlockSpec((COND_ROWS, D_MODEL), lambda l, j: (0, 0)),
            pl.BlockSpec((None, D_MODEL, ADALN_TILE), lambda l, j: (l, 0, j)),
            pl.BlockSpec((None, 1, ADALN_TILE), lambda l, j: (l, 0, j)),
        ],
        out_specs=pl.BlockSpec((None, COND_ROWS, ADALN_TILE), lambda l, j: (l, 0, j)),
        out_shape=jax.ShapeDtypeStruct((depth, COND_ROWS, width), F32),
        compiler_params=_params("arbitrary", "arbitrary"),
        name="adaln",
    )(cond, w_mod, b_mod.reshape(depth, 1, width))


_K_COLS = 2 * KV_WIDTH
IN_TOKEN_COLS = 2 * KV_WIDTH + ATTN_WIDTH + POOL_WIDTH + FOURIER_WIDTH
_A_SPLITS = (0, KV_WIDTH, 2 * KV_WIDTH, 2 * KV_WIDTH + ATTN_WIDTH, 2 * KV_WIDTH + ATTN_WIDTH + POOL_WIDTH,
             IN_TOKEN_COLS)


def _inproj_kernel(x_ref, g_ref, sh_ref, sc_ref, w_ref, cos_ref, sin_ref,
                   k_ref, v_ref, q_ref, u_ref, f_ref):
    h = _modnorm(x_ref[...], g_ref[...], sc_ref[...], sh_ref[...]).astype(BF16)
    cos = cos_ref[...]
    sin = sin_ref[...]

    lane = lax.broadcasted_iota(jnp.int32, cos.shape, 1)
    first = (lane % (HEAD_DIM // 2)) < HEAD_DIM // 4
    low = lane < HEAD_DIM

    def rope(t):
        partner = jnp.where(first, pltpu.roll(t, LANES - HEAD_DIM // 4, 1), pltpu.roll(t, HEAD_DIM // 4, 1))
        return t * cos + partner * sin

    k0, v0, q0, u0, f0, end = _A_SPLITS
    tile = lambda a, j: a[:, j * LANES:(j + 1) * LANES]
    kvq = _dot(h, w_ref[:, k0:u0])
    k = rope(tile(kvq, 0))
    k_swapped = pltpu.roll(k, HEAD_DIM, 1)
    k_ref[:, 0:LANES] = jnp.where(low, k, k_swapped).astype(BF16)
    k_ref[:, LANES:2 * LANES] = jnp.where(low, k_swapped, k).astype(BF16)
    v_ref[...] = tile(kvq, v0 // LANES).astype(BF16)
    scale = HEAD_DIM ** -0.5 * LOG2E
    for j in range(ATTN_WIDTH // LANES):
        q_ref[:, j * LANES:(j + 1) * LANES] = (rope(tile(kvq, q0 // LANES + j)) * scale).astype(BF16)
    uf = _dot(h, w_ref[:, u0:end])
    u_ref[...] = uf[:, 0:f0 - u0]
    for j in range(FOURIER_WIDTH // LANES):
        f_ref[j] = tile(uf, (f0 - u0) // LANES + j)


def _inproj(l, x, mod, gain, w_a, cos_t, sin_t, ctx_row):
    B, T, _ = x.shape
    tm = min(TOKEN_TILE, T)
    tok = lambda w: pl.BlockSpec((None, tm, w), lambda b, i: (b, i, 0))
    return pl.pallas_call(
        _inproj_kernel,
        grid=(B, T // tm),
        in_specs=[
            tok(D_MODEL),
            _layer(gain, l),
            _mod_spec(l, 0, ctx_row), _mod_spec(l, 1, ctx_row),
            pl.BlockSpec((None, D_MODEL, IN_TOKEN_COLS), lambda b, i: (l, 0, 0), pipeline_mode=pl.Buffered(1)),
            pl.BlockSpec((tm, LANES), lambda b, i: (i, 0)),
            pl.BlockSpec((tm, LANES), lambda b, i: (i, 0)),
        ],
        out_specs=[tok(_K_COLS), tok(KV_WIDTH), tok(ATTN_WIDTH), tok(POOL_WIDTH),
                   pl.BlockSpec((None, FOURIER_WIDTH // LANES, tm, LANES), lambda b, i: (b, 0, i, 0))],
        out_shape=[
            jax.ShapeDtypeStruct((B, T, _K_COLS), BF16),
            jax.ShapeDtypeStruct((B, T, KV_WIDTH), BF16),
            jax.ShapeDtypeStruct((B, T, ATTN_WIDTH), BF16),
            jax.ShapeDtypeStruct((B, T, POOL_WIDTH), F32),
            jax.ShapeDtypeStruct((B, FOURIER_WIDTH // LANES, T, LANES), F32),
        ],
        compiler_params=_params("arbitrary", "arbitrary"),
        name="inproj",
    )(x, gain, mod, mod, w_a, cos_t, sin_t)


def _attn_kernel(sink_ref, q_ref, *refs, layer, steps, band):
    if band:
        kp_ref, km_ref, kn_ref, vp_ref, vm_ref, vn_ref, kc_ref, vc_ref, o_ref = refs
    else:
        kc_ref, vc_ref, o_ref = refs
    n = pl.program_id(1)
    rows = Q_PER_KV * BLOCK
    lane = lax.broadcasted_iota(jnp.int32, (BLOCK, LANES), 1)
    low = lane < HEAD_DIM
    rgrp = lax.broadcasted_iota(jnp.int32, (rows, 1), 0) // BLOCK
    if band:
        qi = lax.broadcasted_iota(jnp.int32, (rows, BLOCK), 0) % BLOCK
        kj = lax.broadcasted_iota(jnp.int32, (rows, BLOCK), 1)
        in_prev = kj >= qi
        in_next = kj <= qi
    blk = lambda ref, j, cols: ref[j * BLOCK:(j + 1) * BLOCK, cols]
    every = slice(None)

    def one_head(q0, h, k_parts, v_all, ok_prev, ok_next):
        qs = []
        for t in range(Q_PER_KV // 2):
            c0 = (h * (Q_PER_KV // 2) + t) * LANES
            qt = q_ref[q0:q0 + BLOCK, c0:c0 + LANES]
            qs.append(jnp.where(low, qt, jnp.zeros_like(qt)))
            qs.append(jnp.where(low, jnp.zeros_like(qt), qt))
        qh = jnp.concatenate(qs, axis=0)
        s = _dot_nt(qh, jnp.concatenate(k_parts, axis=0))
        tiles = [s[:, j * LANES:(j + 1) * LANES] for j in range(s.shape[1] // LANES)]
        if band:
            tiles[0] = jnp.where(ok_prev, tiles[0], NEG_INF)
            tiles[2] = jnp.where(ok_next, tiles[2], NEG_INF)
        sink = jnp.zeros((rows, 1), F32)
        for g in range(Q_PER_KV):
            sink = jnp.where(rgrp == g, sink_ref[layer, h * Q_PER_KV + g] * LOG2E, sink)
        m_t = tiles[0]
        for t in tiles[1:]:
            m_t = jnp.maximum(m_t, t)
        m = jnp.maximum(jnp.max(m_t, axis=-1, keepdims=True), sink)
        p = jnp.concatenate([jnp.exp2((t - m).astype(BF16)) for t in tiles], axis=1)
        return _dot(p, v_all), jnp.exp2(sink - m)

    n_sub = ATTN_Q_BLOCKS if band else 1
    for sub in range(n_sub):
        if band:
            pick = lambda pm, mid, nx, j, cols: (blk(pm, 0, cols) if j < 0 else
                                                 blk(nx, 0, cols) if j >= n_sub else blk(mid, j, cols))
            ok_prev = jnp.logical_and(in_prev, n > 0) if sub == 0 else in_prev
            ok_next = jnp.logical_and(in_next, n < steps - 1) if sub == n_sub - 1 else in_next
            v_all = jnp.concatenate([pick(vp_ref, vm_ref, vn_ref, sub + d, every) for d in (-1, 0, 1)]
                                    + [vc_ref[...]], axis=0)
        else:
            ok_prev = ok_next = None
            v_all = vc_ref[...]
        v_low = lax.broadcasted_iota(jnp.int32, v_all.shape, 1) < HEAD_DIM
        ones = jnp.ones_like(v_all)
        v_aug = [jnp.where(v_low, v_all, ones), jnp.where(v_low, ones, v_all)]
        outs = []
        for h in range(N_KV_HEADS):
            ks = slice(h * LANES, (h + 1) * LANES)
            k_parts = [kc_ref[:, ks]]
            if band:
                k_parts = [pick(kp_ref, km_ref, kn_ref, sub + d, ks) for d in (-1, 0, 1)] + k_parts
            outs.append(one_head(sub * BLOCK, h, k_parts, v_aug[h], ok_prev, ok_next))
        (o0, e0), (o1, e1) = outs
        for g in range(Q_PER_KV):
            r = slice(g * BLOCK, (g + 1) * BLOCK)
            num = jnp.where(low, o0[r], o1[r])
            den = pltpu.roll(jnp.where(low, o1[r], o0[r]), HEAD_DIM, 1) + jnp.where(low, e0[r], e1[r])
            o_ref[sub * BLOCK:(sub + 1) * BLOCK, g * LANES:(g + 1) * LANES] = (num / den).astype(BF16)


def _attention(l, q, k, v, kc, vc, sink, band):
    B, T, _ = q.shape
    L = kc.shape[1]
    nq = ATTN_Q_BLOCKS if band else 1
    rows = nq * BLOCK
    steps = T // rows
    nb = T // BLOCK
    qspec = pl.BlockSpec((None, rows, ATTN_WIDTH), lambda b, n: (b, n, 0))
    kprev = lambda w: pl.BlockSpec((None, BLOCK, w), lambda b, n: (b, jnp.maximum(n * nq - 1, 0), 0))
    kmid = lambda w: pl.BlockSpec((None, rows, w), lambda b, n: (b, n, 0))
    knext = lambda w: pl.BlockSpec((None, BLOCK, w), lambda b, n: (b, jnp.minimum((n + 1) * nq, nb - 1), 0))
    cspec = lambda w: pl.BlockSpec((None, L, w), lambda b, n: (b, 0, 0))
    in_specs = [pl.BlockSpec(memory_space=pltpu.SMEM), qspec]
    args = [sink, q]
    if band:
        in_specs += [kprev(_K_COLS), kmid(_K_COLS), knext(_K_COLS),
                     kprev(KV_WIDTH), kmid(KV_WIDTH), knext(KV_WIDTH)]
        args += [k, k, k, v, v, v]
    in_specs += [cspec(_K_COLS), cspec(KV_WIDTH)]
    args += [kc, vc]
    return pl.pallas_call(
        functools.partial(_attn_kernel, layer=l, steps=steps, band=band),
        grid=(B, steps),
        in_specs=in_specs,
        out_specs=pl.BlockSpec((None, rows, ATTN_WIDTH), lambda b, n: (b, n, 0)),
        out_shape=jax.ShapeDtypeStruct((B, T, ATTN_WIDTH), BF16),
        compiler_params=_params("arbitrary", "arbitrary"),
        name="attention_band" if band else "attention_ctx",
    )(*args)


def _pool_kernel(u_ref, w_ref, s_ref, o_ref, pad_ref, *, T, tc):
    zeros = jnp.zeros((POOL_HALO, POOL_WIDTH), F32)
    pad_ref[0:POOL_HALO, :] = zeros
    pad_ref[POOL_HALO + T:POOL_HALO + T + POOL_HALO, :] = zeros
    pad_ref[POOL_HALO:POOL_HALO + T, :] = u_ref[...]
    lane = lax.broadcasted_iota(jnp.int32, (tc, LANES), 1)
    upper = lane >= POOL_GROUP_DIM
    row = lax.broadcasted_iota(jnp.int32, (tc, LANES), 0)
    for c in range(T // tc):
        base = POOL_HALO + c * tc
        t = row + c * tc
        pooled = []
        for tile in range(POOL_WIDTH // LANES):
            w_small = 2 << (2 * tile)
            w_big = 2 * w_small
            cols = slice(tile * LANES, (tile + 1) * LANES)
            load = lambda d: pad_ref[base + d:base + d + tc, cols]
            inner = load(-(w_small // 2))
            for d in range(-(w_small // 2) + 1, w_small // 2):
                inner = inner + load(d)
            outer = load(-(w_big // 2))
            for d in list(range(-(w_big // 2) + 1, -(w_small // 2))) + list(range(w_small // 2, w_big // 2)):
                outer = outer + load(d)
            win = inner + jnp.where(upper, outer, 0.0)
            half = jnp.where(upper, w_big // 2, w_small // 2)
            lo = jnp.maximum(t - half, 0)
            hi = jnp.minimum(t + half, T)
            cnt = (hi - lo).astype(F32)
            pooled.append(win / cnt - load(0))
        p = jnp.concatenate(pooled, axis=1).astype(BF16)
        y = _dot(p, w_ref[...]) * s_ref[...]
        o_ref[c * tc:(c + 1) * tc, :] = y.astype(BF16)


def _pool(l, u, w_bd, scale):
    B, T, _ = u.shape
    tc = min(TOKEN_TILE, T)
    return pl.pallas_call(
        functools.partial(_pool_kernel, T=T, tc=tc),
        grid=(B,),
        in_specs=[
            pl.BlockSpec((None, T, POOL_WIDTH), lambda b: (b, 0, 0)),
            _layer(w_bd, l),
            _layer(scale, l),
        ],
        out_specs=pl.BlockSpec((None, T, POOL_WIDTH), lambda b: (b, 0, 0)),
        out_shape=jax.ShapeDtypeStruct((B, T, POOL_WIDTH), BF16),
        scratch_shapes=[pltpu.VMEM((T + 2 * POOL_HALO, POOL_WIDTH), F32)],
        compiler_params=_params("arbitrary"),
        name="pool",
    )(u, w_bd, scale)


def _fourier_factors(T):
    n1 = 1 << (int(math.log2(T)) // 2)
    return n1, T // n1


def _split_const(a):
    a = jnp.asarray(a, F32)
    hi = a.astype(BF16)
    return hi, (a - hi.astype(F32)).astype(BF16)


def _fourier_tables(T):
    n1, n2 = _fourier_factors(T)
    gd = FOURIER_GROUP_DIM
    t2 = np.arange(n2)[:, None, None]
    k1 = np.arange(n1)[None, :, None]
    t1 = np.arange(n1)[None, None, :]
    ph = -2.0 * np.pi * (t2 * k1 / T + t1 * k1 / n1)
    m1 = np.concatenate([np.cos(ph), np.sin(ph)], axis=1) / math.sqrt(n1)
    a3 = 2.0 * np.pi * np.outer(np.arange(n2), np.arange(n2)) / n2
    c3, s3 = np.cos(a3) / math.sqrt(n2), np.sin(a3) / math.sqrt(n2)
    m3 = np.block([[c3, s3], [-s3, c3]])
    c = np.arange(gd)
    ang = 2.0 * np.pi * np.outer(c, c) / gd
    eye = np.eye(FOURIER_WIDTH // gd)
    cd = np.concatenate([np.kron(eye, np.cos(ang)), np.kron(eye, np.sin(ang))], axis=0) / math.sqrt(gd)
    return _split_const(m1) + _split_const(m3) + _split_const(cd)


def _split(a):
    hi = a.astype(BF16)
    return hi, (a - hi.astype(F32)).astype(BF16)


def _dot3(a, b):
    return _dot(a[0], b[0]) + (_dot(a[1], b[0]) + _dot(a[0], b[1]))


def _fourier_kernel(f_ref, m1h_ref, m1l_ref, m3h_ref, m3l_ref, cdh_ref, cdl_ref, o_ref, a_ref, g_ref,
                    *, n1, n2, tc, unroll):
    T = n1 * n2
    wt = FOURIER_WIDTH // LANES
    lanes = lambda j: slice(j * LANES, (j + 1) * LANES)

    def over_t1(t2, carry):
        rows = pl.ds(t2, n1, stride=n2)
        x = jnp.concatenate([f_ref[j, rows, :] for j in range(wt)], axis=1)
        a = _dot3((m1h_ref[t2], m1l_ref[t2]), _split(x))
        off = pl.multiple_of(t2 * n1, n1)
        for j in range(wt):
            a_ref[j, pl.ds(off, n1), :] = a[:n1, lanes(j)]
            a_ref[wt + j, pl.ds(off, n1), :] = a[n1:, lanes(j)]
        return carry

    lax.fori_loop(0, n2, over_t1, 0, unroll=unroll)

    def over_t2(k1, carry):
        rows = pl.ds(k1, n2, stride=n1)
        b = jnp.concatenate(
            [jnp.concatenate([a_ref[h * wt + j, rows, :] for j in range(wt)], axis=1) for h in range(2)], axis=0)
        g = _dot3((m3h_ref[...], m3l_ref[...]), _split(b))
        for j in range(wt):
            g_ref[j, rows, :] = g[:n2, lanes(j)]
            g_ref[wt + j, rows, :] = g[n2:, lanes(j)]
        return carry

    lax.fori_loop(0, n1, over_t2, 0, unroll=unroll)

    for c in range(T // tc):
        r = slice(c * tc, (c + 1) * tc)
        g = jnp.concatenate([g_ref[j, r, :] for j in range(2 * wt)], axis=1)
        o_ref[r, :] = _dot3(_split(g), (cdh_ref[...], cdl_ref[...])).astype(BF16)


def _fourier(f, tables):
    B, wt, T, _ = f.shape
    n1, n2 = _fourier_factors(T)
    return pl.pallas_call(
        functools.partial(_fourier_kernel, n1=n1, n2=n2, tc=min(TOKEN_TILE, T), unroll=FOURIER_UNROLL),
        grid=(B,),
        in_specs=[pl.BlockSpec((None, wt, T, LANES), lambda b: (b, 0, 0, 0))] + [_resident(t.shape) for t in tables],
        out_specs=pl.BlockSpec((None, T, FOURIER_WIDTH), lambda b: (b, 0, 0)),
        out_shape=jax.ShapeDtypeStruct((B, T, FOURIER_WIDTH), BF16),
        scratch_shapes=[pltpu.VMEM((2 * wt, T, LANES), F32), pltpu.VMEM((2 * wt, T, LANES), F32)],
        compiler_params=_params("arbitrary"),
        name="fourier",
    )(f, *tables)


def _merge_kernel(x_ref, o_ref, p_ref, f_ref, g_ref, sh_ref, sc_ref, gt_ref,
                  wg_ref, wa_ref, wp_ref, wf_ref, wo_ref, out_ref, y_ref):
    x = x_ref[...]
    h = _modnorm(x, g_ref[...], sc_ref[...], sh_ref[...]).astype(BF16)
    o = o_ref[...]
    p = p_ref[...]
    f = f_ref[...]
    for j in range(D_MODEL // MERGE_CHUNK):
        c = slice(j * MERGE_CHUNK, (j + 1) * MERGE_CHUNK)
        gate = lambda b: jax.nn.sigmoid(_dot(h, wg_ref[:, b * D_MODEL + j * MERGE_CHUNK:
                                                        b * D_MODEL + (j + 1) * MERGE_CHUNK]))
        y = (gate(0) * _dot(o, wa_ref[:, c]) + gate(1) * _dot(p, wp_ref[:, c])
             + gate(2) * _dot(f, wf_ref[:, c]))
        y_ref[:, c] = y.astype(BF16)
    out_ref[...] = x + gt_ref[...] * _dot(y_ref[...], wo_ref[...])


def _merge(l, x, o, p, f, mod, gain, wg, wa, wp, wf, wo, ctx_row):
    B, T, _ = x.shape
    tm = min(MERGE_TILE, T)
    tok = lambda w: pl.BlockSpec((None, tm, w), lambda b, i: (b, i, 0))
    return pl.pallas_call(
        _merge_kernel,
        grid=(B, T // tm),
        in_specs=[
            tok(D_MODEL), tok(ATTN_WIDTH), tok(POOL_WIDTH), tok(FOURIER_WIDTH),
            _layer(gain, l),
            _mod_spec(l, 0, ctx_row), _mod_spec(l, 1, ctx_row), _mod_spec(l, 2, ctx_row),
            _layer(wg, l), _layer(wa, l), _layer(wp, l), _layer(wf, l), _layer(wo, l),
        ],
        out_specs=tok(D_MODEL),
        out_shape=jax.ShapeDtypeStruct((B, T, D_MODEL), F32),
        scratch_shapes=[pltpu.VMEM((tm, D_MODEL), BF16)],
        compiler_params=_params("arbitrary", "arbitrary"),
        name="merge",
    )(x, o, p, f, gain, mod, mod, mod, wg, wa, wp, wf, wo)


def _ffn_kernel(x_ref, xp_ref, xn_ref, g_ref, sh_ref, sc_ref, gt_ref, wu_ref, cw_ref, wd_ref, gf_ref,
                out_ref, h_ref, a_ref, *, tm, final):
    i = pl.program_id(1)
    last = pl.num_programs(1) - 1
    g, sh, sc = g_ref[...], sh_ref[...], sc_ref[...]
    x = x_ref[...]
    H = SUBLANES
    h_ref[H:H + tm, :] = _modnorm(x, g, sc, sh).astype(BF16)
    keep_prev = (i > 0).astype(F32)
    keep_next = (i < last).astype(F32)
    h_ref[0:H, :] = (_modnorm(xp_ref[...], g, sc, sh) * keep_prev).astype(BF16)
    h_ref[H + tm:H + tm + H, :] = (_modnorm(xn_ref[...], g, sc, sh) * keep_next).astype(BF16)
    hf = h_ref[...]
    rows = tm + 2 * H

    def conv(up, cols):
        w = cw_ref[:, cols]
        y = (pltpu.roll(up, 1, 0) * w[0:1] + up * w[1:2] + pltpu.roll(up, rows - 1, 0) * w[2:3])
        return y[H:H + tm]

    for j in range(D_FF // FF_CHUNK):
        cv = slice(j * FF_CHUNK, (j + 1) * FF_CHUNK)
        cg = slice(D_FF + j * FF_CHUNK, D_FF + (j + 1) * FF_CHUNK)
        val = conv(_dot(hf, wu_ref[:, cv]), cv)
        gate = conv(_dot(hf, wu_ref[:, cg]), cg)
        a_ref[:, cv] = (val * (gate * jax.nn.sigmoid(gate))).astype(BF16)
    y = x + gt_ref[...] * _dot(a_ref[...], wd_ref[...])
    if final:
        ms = jnp.mean(y * y, axis=-1, keepdims=True)
        y = y * lax.rsqrt(ms + EPS) * gf_ref[...]
    out_ref[...] = y


def _ffn(l, x, mod, gain, wu, cw, wd, gain_final, ctx_row, final):
    B, T, _ = x.shape
    tm = min(FFN_TILE, T)
    per = tm // SUBLANES
    nblk = T // SUBLANES
    tok = pl.BlockSpec((None, tm, D_MODEL), lambda b, i: (b, i, 0))
    prev = pl.BlockSpec((None, SUBLANES, D_MODEL), lambda b, i: (b, jnp.maximum(i * per - 1, 0), 0))
    nxt = pl.BlockSpec((None, SUBLANES, D_MODEL), lambda b, i: (b, jnp.minimum((i + 1) * per, nblk - 1), 0))
    return pl.pallas_call(
        functools.partial(_ffn_kernel, tm=tm, final=final),
        grid=(B, T // tm),
        in_specs=[
            tok, prev, nxt,
            _layer(gain, l),
            _mod_spec(l, 3, ctx_row), _mod_spec(l, 4, ctx_row), _mod_spec(l, 5, ctx_row),
            _layer(wu, l), _layer(cw, l), _layer(wd, l),
            _resident((1, D_MODEL)),
        ],
        out_specs=tok,
        out_shape=jax.ShapeDtypeStruct((B, T, D_MODEL), F32),
        scratch_shapes=[pltpu.VMEM((tm + 2 * SUBLANES, D_MODEL), BF16), pltpu.VMEM((tm, D_FF), BF16)],
        compiler_params=_params("arbitrary", "arbitrary"),
        name="ffn_final" if final else "ffn",
    )(x, x, x, gain, mod, mod, mod, wu, cw, wd, gain_final)


def _rope_tables(T):
    rows = T // GRID_W
    row = np.repeat(np.arange(rows), GRID_W).astype(np.float64)
    col = np.tile(np.arange(GRID_W), rows).astype(np.float64)
    n_freq = HEAD_DIM // 4
    inv_freq = ROPE_BASE ** (-np.arange(n_freq) / n_freq)
    ar, ac = row[:, None] * inv_freq[None, :], col[:, None] * inv_freq[None, :]
    cos = np.concatenate([np.cos(ar), np.cos(ar), np.cos(ac), np.cos(ac)], axis=1)
    sin = np.concatenate([-np.sin(ar), np.sin(ar), -np.sin(ac), np.sin(ac)], axis=1)
    reps = LANES // HEAD_DIM
    return jnp.asarray(np.tile(cos, (1, reps)), F32), jnp.asarray(np.tile(sin, (1, reps)), F32)


def kernel(x, c, ctx, c_ctx, w_mod, b_mod, norm_mix, norm_ffn, w_in, attn_sink, pool_w, pool_scale,
           w_br_attn, w_br_pool, w_br_four, w_out, w_up, conv_w, w_down, norm_final):
    B, T, _ = x.shape
    L = ctx.shape[1]
    depth = w_mod.shape[0]
    assert B < COND_ROWS and T % max(MERGE_TILE, FFN_TILE) == 0 and T % (ATTN_Q_BLOCKS * BLOCK) == 0
    assert L % BLOCK == 0

    cond = jnp.concatenate([c, c_ctx[None], jnp.zeros((COND_ROWS - B - 1, D_MODEL), F32)], axis=0)
    mod = _adaln(cond, w_mod, b_mod).reshape(depth, COND_ROWS, N_MOD, 1, D_MODEL)

    cos_t, sin_t = _rope_tables(T)
    cos_c = jnp.ones((B * L, LANES), F32)
    sin_c = jnp.zeros((B * L, LANES), F32)
    four_lat = _fourier_tables(T)
    four_ctx = _fourier_tables(L)

    w_a = w_in.astype(BF16)
    w_g = w_in[:, :, IN_TOKEN_COLS:].astype(BF16)
    w_a_br = (w_br_attn.reshape(depth, N_KV_HEADS, Q_PER_KV, HEAD_DIM, D_MODEL)
              .transpose(0, 2, 1, 3, 4).reshape(depth, ATTN_WIDTH, D_MODEL).astype(BF16))
    w_p_br = w_br_pool.astype(BF16)
    w_f_br = w_br_four.astype(BF16)
    w_o = w_out.astype(BF16)
    groups = pool_w.shape[1]
    w_pool = jnp.einsum("lgab,gh->lgahb", pool_w, jnp.eye(groups, dtype=pool_w.dtype)).reshape(
        depth, POOL_WIDTH, POOL_WIDTH).astype(BF16)
    p_scale = pool_scale.reshape(depth, 1, POOL_WIDTH)
    w_u = w_up.astype(BF16)
    w_d = w_down.astype(BF16)
    g_mix = norm_mix.reshape(depth, 1, D_MODEL)
    g_ffn = norm_ffn.reshape(depth, 1, D_MODEL)
    gain_final = norm_final.reshape(1, D_MODEL)

    xc = ctx
    for l in range(depth):
        last = l == depth - 1
        k, v, q, u, f = _inproj(l, x, mod, g_mix, w_a, cos_t, sin_t, None)
        flat = lambda a: a.reshape((1, B * L) + a.shape[2:])
        kc, vc, qc, uc, fc = _inproj(l, flat(xc), mod, g_mix, w_a, cos_c, sin_c, B)
        kc, vc, qc, uc = (a.reshape((B, L) + a.shape[2:]) for a in (kc, vc, qc, uc))
        fc = fc.reshape(FOURIER_WIDTH // LANES, B, L, LANES).transpose(1, 0, 2, 3)
        o = _attention(l, q, k, v, kc, vc, attn_sink, True)
        x = _merge(l, x, o, _pool(l, u, w_pool, p_scale), _fourier(f, four_lat), mod, g_mix,
                   w_g, w_a_br, w_p_br, w_f_br, w_o, None)
        if not last:
            oc = _attention(l, qc, None, None, kc, vc, attn_sink, False)
            xc = _merge(l, flat(xc), flat(oc), flat(_pool(l, uc, w_pool, p_scale)), flat(_fourier(fc, four_ctx)),
                        mod, g_mix, w_g, w_a_br, w_p_br, w_f_br, w_o, B).reshape(B, L, D_MODEL)
        x = _ffn(l, x, mod, g_ffn, w_u, conv_w, w_d, gain_final, None, last)
        if not last:
            xc = _ffn(l, xc, mod, g_ffn, w_u, conv_w, w_d, gain_final, B, False)
    return x
```

```python
import functools
import math

import numpy as np
import jax
import jax.numpy as jnp
from jax import lax
from jax.experimental import pallas as pl
from jax.experimental.pallas import tpu as pltpu

D_MODEL = 1024
N_Q_HEADS = 8
N_KV_HEADS = 2
HEAD_DIM = 64
Q_PER_KV = N_Q_HEADS // N_KV_HEADS
ATTN_WIDTH = N_Q_HEADS * HEAD_DIM
KV_WIDTH = N_KV_HEADS * HEAD_DIM
BLOCK = 128
GRID_W = 64
ROPE_BASE = 10000.0
POOL_WIDTH = 256
POOL_GROUP_DIM = 64
FOURIER_WIDTH = 256
FOURIER_GROUP_DIM = 64
D_FF = 2816
N_MOD = 6
EPS = 1e-6
NEG_INF = -1e30
LOG2E = math.log2(math.e)

LANES = 128
SUBLANES = 8
VMEM_LIMIT_BYTES = 56 * 1024 * 1024

TOKEN_TILE = 512
MERGE_TILE = 1024
FFN_TILE = 1024
ATTN_Q_BLOCKS = 8
FF_CHUNK = 256
MERGE_CHUNK = 256
FOURIER_ROW_PAD = 8
FOURIER_UNROLL = 8
POOL_HALO = 8
ADALN_TILE = 1536
COND_ROWS = 8

F32 = jnp.float32
BF16 = jnp.bfloat16


def _dot(a, b):
    return jnp.dot(a, b, preferred_element_type=F32)


def _dot_nt(a, b):
    return lax.dot_general(a, b, (((1,), (1,)), ((), ())), preferred_element_type=F32)


def _modnorm(x, g, sc, sh):
    ms = jnp.mean(x * x, axis=-1, keepdims=True)
    y = x * lax.rsqrt(ms + EPS) * g
    return y * (1.0 + sc) + sh


def _resident(shape):
    nd = len(shape)
    return pl.BlockSpec(shape, lambda *_: (0,) * nd, pipeline_mode=pl.Buffered(1))


def _layer(arr, l):
    nd = arr.ndim - 1
    return pl.BlockSpec((None,) + arr.shape[1:], lambda *_: (l,) + (0,) * nd, pipeline_mode=pl.Buffered(1))


def _mod_spec(l, j, ctx_row):
    if ctx_row is None:
        return pl.BlockSpec((None, None, None, 1, D_MODEL), lambda b, i: (l, b, j, 0, 0))
    return pl.BlockSpec((None, None, None, 1, D_MODEL), lambda b, i: (l, ctx_row, j, 0, 0))


def _params(*sem):
    return pltpu.CompilerParams(dimension_semantics=sem, vmem_limit_bytes=VMEM_LIMIT_BYTES)


def _adaln_kernel(c_ref, w_ref, b_ref, o_ref):
    c = c_ref[...]
    a = (c * jax.nn.sigmoid(c)).astype(BF16)
    o_ref[...] = _dot(a, w_ref[...].astype(BF16)) + b_ref[...]


def _adaln(cond, w_mod, b_mod):
    depth, _, width = w_mod.shape
    return pl.pallas_call(
        _adaln_kernel,
        grid=(depth, width // ADALN_TILE),
        in_specs=[
            pl.BlockSpec((COND_ROWS, D_MODEL), lambda l, j: (0, 0)),
            pl.BlockSpec((None, D_MODEL, ADALN_TILE), lambda l, j: (l, 0, j)),
            pl.BlockSpec((None, 1, ADALN_TILE), lambda l, j: (l, 0, j)),
        ],
        out_specs=pl.BlockSpec((None, COND_ROWS, ADALN_TILE), lambda l, j: (l, 0, j)),
        out_shape=jax.ShapeDtypeStruct((depth, COND_ROWS, width), F32),
        compiler_params=_params("arbitrary", "arbitrary"),
        name="adaln",
    )(cond, w_mod, b_mod.reshape(depth, 1, width))


_K_COLS = 2 * KV_WIDTH
IN_TOKEN_COLS = 2 * KV_WIDTH + ATTN_WIDTH + POOL_WIDTH + FOURIER_WIDTH
_A_SPLITS = (0, KV_WIDTH, 2 * KV_WIDTH, 2 * KV_WIDTH + ATTN_WIDTH, 2 * KV_WIDTH + ATTN_WIDTH + POOL_WIDTH,
             IN_TOKEN_COLS)


def _inproj_kernel(x_ref, g_ref, sh_ref, sc_ref, w_ref, cos_ref, sin_ref,
                   k_ref, v_ref, q_ref, u_ref, f_ref):
    h = _modnorm(x_ref[...], g_ref[...], sc_ref[...], sh_ref[...]).astype(BF16)
    cos = cos_ref[...]
    sin = sin_ref[...]

    lane = lax.broadcasted_iota(jnp.int32, cos.shape, 1)
    first = (lane % (HEAD_DIM // 2)) < HEAD_DIM // 4
    low = lane < HEAD_DIM

    def rope(t):
        partner = jnp.where(first, pltpu.roll(t, LANES - HEAD_DIM // 4, 1), pltpu.roll(t, HEAD_DIM // 4, 1))
        return t * cos + partner * sin

    k0, v0, q0, u0, f0, end = _A_SPLITS
    tile = lambda a, j: a[:, j * LANES:(j + 1) * LANES]
    kvq = _dot(h, w_ref[:, k0:u0])
    k = rope(tile(kvq, 0))
    k_swapped = pltpu.roll(k, HEAD_DIM, 1)
    k_ref[:, 0:LANES] = jnp.where(low, k, k_swapped).astype(BF16)
    k_ref[:, LANES:2 * LANES] = jnp.where(low, k_swapped, k).astype(BF16)
    v_ref[...] = tile(kvq, v0 // LANES).astype(BF16)
    scale = HEAD_DIM ** -0.5 * LOG2E
    for j in range(ATTN_WIDTH // LANES):
        q_ref[:, j * LANES:(j + 1) * LANES] = (rope(tile(kvq, q0 // LANES + j)) * scale).astype(BF16)
    uf = _dot(h, w_ref[:, u0:end])
    u_ref[...] = uf[:, 0:f0 - u0]
    for j in range(FOURIER_WIDTH // LANES):
        f_ref[j] = tile(uf, (f0 - u0) // LANES + j)


def _inproj(l, x, mod, gain, w_a, cos_t, sin_t, ctx_row):
    B, T, _ = x.shape
    tm = min(TOKEN_TILE, T)
    tok = lambda w: pl.BlockSpec((None, tm, w), lambda b, i: (b, i, 0))
    return pl.pallas_call(
        _inproj_kernel,
        grid=(B, T // tm),
        in_specs=[
            tok(D_MODEL),
            _layer(gain, l),
            _mod_spec(l, 0, ctx_row), _mod_spec(l, 1, ctx_row),
            pl.BlockSpec((None, D_MODEL, IN_TOKEN_COLS), lambda b, i: (l, 0, 0), pipeline_mode=pl.Buffered(1)),
            pl.BlockSpec((tm, LANES), lambda b, i: (i, 0)),
            pl.BlockSpec((tm, LANES), lambda b, i: (i, 0)),
        ],
        out_specs=[tok(_K_COLS), tok(KV_WIDTH), tok(ATTN_WIDTH), tok(POOL_WIDTH),
                   pl.BlockSpec((None, FOURIER_WIDTH // LANES, tm, LANES), lambda b, i: (b, 0, i, 0))],
        out_shape=[
            jax.ShapeDtypeStruct((B, T, _K_COLS), BF16),
            jax.ShapeDtypeStruct((B, T, KV_WIDTH), BF16),
            jax.ShapeDtypeStruct((B, T, ATTN_WIDTH), BF16),
            jax.ShapeDtypeStruct((B, T, POOL_WIDTH), F32),
            jax.ShapeDtypeStruct((B, FOURIER_WIDTH // LANES, T, LANES), F32),
        ],
        compiler_params=_params("arbitrary", "arbitrary"),
        name="inproj",
    )(x, gain, mod, mod, w_a, cos_t, sin_t)


def _attn_kernel(sink_ref, q_ref, *refs, layer, steps, band):
    if band:
        kp_ref, km_ref, kn_ref, vp_ref, vm_ref, vn_ref, kc_ref, vc_ref, o_ref = refs
    else:
        kc_ref, vc_ref, o_ref = refs
    n = pl.program_id(1)
    rows = Q_PER_KV * BLOCK
    lane = lax.broadcasted_iota(jnp.int32, (BLOCK, LANES), 1)
    low = lane < HEAD_DIM
    rgrp = lax.broadcasted_iota(jnp.int32, (rows, 1), 0) // BLOCK
    if band:
        qi = lax.broadcasted_iota(jnp.int32, (rows, BLOCK), 0) % BLOCK
        kj = lax.broadcasted_iota(jnp.int32, (rows, BLOCK), 1)
        in_prev = kj >= qi
        in_next = kj <= qi
    blk = lambda ref, j, cols: ref[j * BLOCK:(j + 1) * BLOCK, cols]
    every = slice(None)

    def one_head(q0, h, k_parts, v_all, ok_prev, ok_next):
        qs = []
        for t in range(Q_PER_KV // 2):
            c0 = (h * (Q_PER_KV // 2) + t) * LANES
            qt = q_ref[q0:q0 + BLOCK, c0:c0 + LANES]
            qs.append(jnp.where(low, qt, jnp.zeros_like(qt)))
            qs.append(jnp.where(low, jnp.zeros_like(qt), qt))
        qh = jnp.concatenate(qs, axis=0)
        s = _dot_nt(qh, jnp.concatenate(k_parts, axis=0))
        tiles = [s[:, j * LANES:(j + 1) * LANES] for j in range(s.shape[1] // LANES)]
        if band:
            tiles[0] = jnp.where(ok_prev, tiles[0], NEG_INF)
            tiles[2] = jnp.where(ok_next, tiles[2], NEG_INF)
        sink = jnp.zeros((rows, 1), F32)
        for g in range(Q_PER_KV):
            sink = jnp.where(rgrp == g, sink_ref[layer, h * Q_PER_KV + g] * LOG2E, sink)
        m_t = tiles[0]
        for t in tiles[1:]:
            m_t = jnp.maximum(m_t, t)
        m = jnp.maximum(jnp.max(m_t, axis=-1, keepdims=True), sink)
        p = jnp.concatenate([jnp.exp2((t - m).astype(BF16)) for t in tiles], axis=1)
        return _dot(p, v_all), jnp.exp2(sink - m)

    n_sub = ATTN_Q_BLOCKS if band else 1
    for sub in range(n_sub):
        if band:
            pick = lambda pm, mid, nx, j, cols: (blk(pm, 0, cols) if j < 0 else
                                                 blk(nx, 0, cols) if j >= n_sub else blk(mid, j, cols))
            ok_prev = jnp.logical_and(in_prev, n > 0) if sub == 0 else in_prev
            ok_next = jnp.logical_and(in_next, n < steps - 1) if sub == n_sub - 1 else in_next
            v_all = jnp.concatenate([pick(vp_ref, vm_ref, vn_ref, sub + d, every) for d in (-1, 0, 1)]
                                    + [vc_ref[...]], axis=0)
        else:
            ok_prev = ok_next = None
            v_all = vc_ref[...]
        v_low = lax.broadcasted_iota(jnp.int32, v_all.shape, 1) < HEAD_DIM
        ones = jnp.ones_like(v_all)
        v_aug = [jnp.where(v_low, v_all, ones), jnp.where(v_low, ones, v_all)]
        outs = []
        for h in range(N_KV_HEADS):
            ks = slice(h * LANES, (h + 1) * LANES)
            k_parts = [kc_ref[:, ks]]
            if band:
                k_parts = [pick(kp_ref, km_ref, kn_ref, sub + d, ks) for d in (-1, 0, 1)] + k_parts
            outs.append(one_head(sub * BLOCK, h, k_parts, v_aug[h], ok_prev, ok_next))
        (o0, e0), (o1, e1) = outs
        for g in range(Q_PER_KV):
            r = slice(g * BLOCK, (g + 1) * BLOCK)
            num = jnp.where(low, o0[r], o1[r])
            den = pltpu.roll(jnp.where(low, o1[r], o0[r]), HEAD_DIM, 1) + jnp.where(low, e0[r], e1[r])
            o_ref[sub * BLOCK:(sub + 1) * BLOCK, g * LANES:(g + 1) * LANES] = (num / den).astype(BF16)


def _attention(l, q, k, v, kc, vc, sink, band):
    B, T, _ = q.shape
    L = kc.shape[1]
    nq = ATTN_Q_BLOCKS if band else 1
    rows = nq * BLOCK
    steps = T // rows
    nb = T // BLOCK
    qspec = pl.BlockSpec((None, rows, ATTN_WIDTH), lambda b, n: (b, n, 0))
    kprev = lambda w: pl.BlockSpec((None, BLOCK, w), lambda b, n: (b, jnp.maximum(n * nq - 1, 0), 0))
    kmid = lambda w: pl.BlockSpec((None, rows, w), lambda b, n: (b, n, 0))
    knext = lambda w: pl.BlockSpec((None, BLOCK, w), lambda b, n: (b, jnp.minimum((n + 1) * nq, nb - 1), 0))
    cspec = lambda w: pl.BlockSpec((None, L, w), lambda b, n: (b, 0, 0))
    in_specs = [pl.BlockSpec(memory_space=pltpu.SMEM), qspec]
    args = [sink, q]
    if band:
        in_specs += [kprev(_K_COLS), kmid(_K_COLS), knext(_K_COLS),
                     kprev(KV_WIDTH), kmid(KV_WIDTH), knext(KV_WIDTH)]
        args += [k, k, k, v, v, v]
    in_specs += [cspec(_K_COLS), cspec(KV_WIDTH)]
    args += [kc, vc]
    return pl.pallas_call(
        functools.partial(_attn_kernel, layer=l, steps=steps, band=band),
        grid=(B, steps),
        in_specs=in_specs,
        out_specs=pl.BlockSpec((None, rows, ATTN_WIDTH), lambda b, n: (b, n, 0)),
        out_shape=jax.ShapeDtypeStruct((B, T, ATTN_WIDTH), BF16),
        compiler_params=_params("arbitrary", "arbitrary"),
        name="attention_band" if band else "attention_ctx",
    )(*args)


def _pool_kernel(u_ref, w_ref, s_ref, o_ref, pad_ref, *, T, tc):
    zeros = jnp.zeros((POOL_HALO, POOL_WIDTH), F32)
    pad_ref[0:POOL_HALO, :] = zeros
    pad_ref[POOL_HALO + T:POOL_HALO + T + POOL_HALO, :] = zeros
    pad_ref[POOL_HALO:POOL_HALO + T, :] = u_ref[...]
    lane = lax.broadcasted_iota(jnp.int32, (tc, LANES), 1)
    upper = lane >= POOL_GROUP_DIM
    row = lax.broadcasted_iota(jnp.int32, (tc, LANES), 0)
    for c in range(T // tc):
        base = POOL_HALO + c * tc
        t = row + c * tc
        pooled = []
        for tile in range(POOL_WIDTH // LANES):
            w_small = 2 << (2 * tile)
            w_big = 2 * w_small
            cols = slice(tile * LANES, (tile + 1) * LANES)
            load = lambda d: pad_ref[base + d:base + d + tc, cols]
            inner = load(-(w_small // 2))
            for d in range(-(w_small // 2) + 1, w_small // 2):
                inner = inner + load(d)
            outer = load(-(w_big // 2))
            for d in list(range(-(w_big // 2) + 1, -(w_small // 2))) + list(range(w_small // 2, w_big // 2)):
                outer = outer + load(d)
            win = inner + jnp.where(upper, outer, 0.0)
            half = jnp.where(upper, w_big // 2, w_small // 2)
            lo = jnp.maximum(t - half, 0)
            hi = jnp.minimum(t + half, T)
            cnt = (hi - lo).astype(F32)
            pooled.append(win / cnt - load(0))
        p = jnp.concatenate(pooled, axis=1).astype(BF16)
        y = _dot(p, w_ref[...]) * s_ref[...]
        o_ref[c * tc:(c + 1) * tc, :] = y.astype(BF16)


def _pool(l, u, w_bd, scale):
    B, T, _ = u.shape
    tc = min(TOKEN_TILE, T)
    return pl.pallas_call(
        functools.partial(_pool_kernel, T=T, tc=tc),
        grid=(B,),
        in_specs=[
            pl.BlockSpec((None, T, POOL_WIDTH), lambda b: (b, 0, 0)),
            _layer(w_bd, l),
            _layer(scale, l),
        ],
        out_specs=pl.BlockSpec((None, T, POOL_WIDTH), lambda b: (b, 0, 0)),
        out_shape=jax.ShapeDtypeStruct((B, T, POOL_WIDTH), BF16),
        scratch_shapes=[pltpu.VMEM((T + 2 * POOL_HALO, POOL_WIDTH), F32)],
        compiler_params=_params("arbitrary"),
        name="pool",
    )(u, w_bd, scale)


def _fourier_factors(T):
    n1 = 1 << (int(math.log2(T)) // 2)
    return n1, T // n1


def _split_const(a):
    a = jnp.asarray(a, F32)
    hi = a.astype(BF16)
    return hi, (a - hi.astype(F32)).astype(BF16)


def _fourier_tables(T):
    n1, n2 = _fourier_factors(T)
    gd = FOURIER_GROUP_DIM
    t2 = np.arange(n2)[:, None, None]
    k1 = np.arange(n1)[None, :, None]
    t1 = np.arange(n1)[None, None, :]
    ph = -2.0 * np.pi * (t2 * k1 / T + t1 * k1 / n1)
    m1 = np.concatenate([np.cos(ph), np.sin(ph)], axis=1) / math.sqrt(n1)
    a3 = 2.0 * np.pi * np.outer(np.arange(n2), np.arange(n2)) / n2
    c3, s3 = np.cos(a3) / math.sqrt(n2), np.sin(a3) / math.sqrt(n2)
    m3 = np.block([[c3, s3], [-s3, c3]])
    c = np.arange(gd)
    ang = 2.0 * np.pi * np.outer(c, c) / gd
    eye = np.eye(FOURIER_WIDTH // gd)
    cd = np.concatenate([np.kron(eye, np.cos(ang)), np.kron(eye, np.sin(ang))], axis=0) / math.sqrt(gd)
    return _split_const(m1) + _split_const(m3) + _split_const(cd)


def _split(a):
    hi = a.astype(BF16)
    return hi, (a - hi.astype(F32)).astype(BF16)


def _dot3(a, b):
    return _dot(a[0], b[0]) + (_dot(a[1], b[0]) + _dot(a[0], b[1]))


def _fourier_kernel(f_ref, m1h_ref, m1l_ref, m3h_ref, m3l_ref, cdh_ref, cdl_ref, o_ref, a_ref, g_ref,
                    *, n1, n2, tc, unroll):
    T = n1 * n2
    wt = FOURIER_WIDTH // LANES
    lanes = lambda j: slice(j * LANES, (j + 1) * LANES)
    pitch = n1 + FOURIER_ROW_PAD

    def over_t1(t2, carry):
        rows = pl.ds(t2, n1, stride=n2)
        x = jnp.concatenate([f_ref[j, rows, :] for j in range(wt)], axis=1)
        a = _dot3((m1h_ref[t2], m1l_ref[t2]), _split(x))
        off = pl.multiple_of(t2 * pitch, SUBLANES)
        for j in range(wt):
            a_ref[j, pl.ds(off, n1), :] = a[:n1, lanes(j)]
            a_ref[wt + j, pl.ds(off, n1), :] = a[n1:, lanes(j)]
        return carry

    lax.fori_loop(0, n2, over_t1, 0, unroll=unroll)

    def over_t2(k1, carry):
        rows = pl.ds(k1, n2, stride=pitch)
        b = jnp.concatenate(
            [jnp.concatenate([a_ref[h * wt + j, rows, :] for j in range(wt)], axis=1) for h in range(2)], axis=0)
        g = _dot3((m3h_ref[...], m3l_ref[...]), _split(b))
        for j in range(wt):
            g_ref[j, rows, :] = g[:n2, lanes(j)]
            g_ref[wt + j, rows, :] = g[n2:, lanes(j)]
        return carry

    lax.fori_loop(0, n1, over_t2, 0, unroll=unroll)

    per = tc // n1
    for c in range(T // tc):
        groups = [slice((c * per + i) * pitch, (c * per + i) * pitch + n1) for i in range(per)]
        g = jnp.concatenate([jnp.concatenate([g_ref[j, r, :] for r in groups], axis=0) for j in range(2 * wt)],
                            axis=1)
        o_ref[c * tc:(c + 1) * tc, :] = _dot3(_split(g), (cdh_ref[...], cdl_ref[...])).astype(BF16)


def _fourier(f, tables):
    B, wt, T, _ = f.shape
    n1, n2 = _fourier_factors(T)
    return pl.pallas_call(
        functools.partial(_fourier_kernel, n1=n1, n2=n2, tc=min(TOKEN_TILE, T), unroll=FOURIER_UNROLL),
        grid=(B,),
        in_specs=[pl.BlockSpec((None, wt, T, LANES), lambda b: (b, 0, 0, 0))] + [_resident(t.shape) for t in tables],
        out_specs=pl.BlockSpec((None, T, FOURIER_WIDTH), lambda b: (b, 0, 0)),
        out_shape=jax.ShapeDtypeStruct((B, T, FOURIER_WIDTH), BF16),
        scratch_shapes=[pltpu.VMEM((2 * wt, n2 * (n1 + FOURIER_ROW_PAD), LANES), F32)] * 2,
        compiler_params=_params("arbitrary"),
        name="fourier",
    )(f, *tables)


def _merge_kernel(x_ref, o_ref, p_ref, f_ref, g_ref, sh_ref, sc_ref, gt_ref,
                  wg_ref, wa_ref, wp_ref, wf_ref, wo_ref, out_ref, y_ref):
    x = x_ref[...]
    h = _modnorm(x, g_ref[...], sc_ref[...], sh_ref[...]).astype(BF16)
    o = o_ref[...]
    p = p_ref[...]
    f = f_ref[...]
    for j in range(D_MODEL // MERGE_CHUNK):
        c = slice(j * MERGE_CHUNK, (j + 1) * MERGE_CHUNK)
        gcol = lambda b: IN_TOKEN_COLS + b * D_MODEL + j * MERGE_CHUNK
        gate = lambda b: jax.nn.sigmoid(_dot(h, wg_ref[:, gcol(b):gcol(b) + MERGE_CHUNK]))
        y = (gate(0) * _dot(o, wa_ref[:, c]) + gate(1) * _dot(p, wp_ref[:, c])
             + gate(2) * _dot(f, wf_ref[:, c]))
        y_ref[:, c] = y.astype(BF16)
    out_ref[...] = x + gt_ref[...] * _dot(y_ref[...], wo_ref[...])


def _merge(l, x, o, p, f, mod, gain, wg, wa, wp, wf, wo, ctx_row):
    B, T, _ = x.shape
    tm = min(MERGE_TILE, T)
    tok = lambda w: pl.BlockSpec((None, tm, w), lambda b, i: (b, i, 0))
    return pl.pallas_call(
        _merge_kernel,
        grid=(B, T // tm),
        in_specs=[
            tok(D_MODEL), tok(ATTN_WIDTH), tok(POOL_WIDTH), tok(FOURIER_WIDTH),
            _layer(gain, l),
            _mod_spec(l, 0, ctx_row), _mod_spec(l, 1, ctx_row), _mod_spec(l, 2, ctx_row),
            _layer(wg, l), _layer(wa, l), _layer(wp, l), _layer(wf, l), _layer(wo, l),
        ],
        out_specs=tok(D_MODEL),
        out_shape=jax.ShapeDtypeStruct((B, T, D_MODEL), F32),
        scratch_shapes=[pltpu.VMEM((tm, D_MODEL), BF16)],
        compiler_params=_params("arbitrary", "arbitrary"),
        name="merge",
    )(x, o, p, f, gain, mod, mod, mod, wg, wa, wp, wf, wo)


def _ffn_kernel(x_ref, xp_ref, xn_ref, g_ref, sh_ref, sc_ref, gt_ref, wu_ref, cw_ref, wd_ref, gf_ref,
                out_ref, h_ref, a_ref, *, tm, final):
    i = pl.program_id(1)
    last = pl.num_programs(1) - 1
    g, sh, sc = g_ref[...], sh_ref[...], sc_ref[...]
    x = x_ref[...]
    H = SUBLANES
    h_ref[H:H + tm, :] = _modnorm(x, g, sc, sh).astype(BF16)
    keep_prev = (i > 0).astype(F32)
    keep_next = (i < last).astype(F32)
    h_ref[0:H, :] = (_modnorm(xp_ref[...], g, sc, sh) * keep_prev).astype(BF16)
    h_ref[H + tm:H + tm + H, :] = (_modnorm(xn_ref[...], g, sc, sh) * keep_next).astype(BF16)
    hf = h_ref[...]
    rows = tm + 2 * H

    def conv(up, cols):
        w = cw_ref[:, cols]
        y = (pltpu.roll(up, 1, 0) * w[0:1] + up * w[1:2] + pltpu.roll(up, rows - 1, 0) * w[2:3])
        return y[H:H + tm]

    for j in range(D_FF // FF_CHUNK):
        cv = slice(j * FF_CHUNK, (j + 1) * FF_CHUNK)
        cg = slice(D_FF + j * FF_CHUNK, D_FF + (j + 1) * FF_CHUNK)
        val = conv(_dot(hf, wu_ref[:, cv]), cv)
        gate = conv(_dot(hf, wu_ref[:, cg]), cg)
        a_ref[:, cv] = (val * (gate * jax.nn.sigmoid(gate))).astype(BF16)
    y = x + gt_ref[...] * _dot(a_ref[...], wd_ref[...])
    if final:
        ms = jnp.mean(y * y, axis=-1, keepdims=True)
        y = y * lax.rsqrt(ms + EPS) * gf_ref[...]
    out_ref[...] = y


def _ffn(l, x, mod, gain, wu, cw, wd, gain_final, ctx_row, final):
    B, T, _ = x.shape
    tm = min(FFN_TILE, T)
    per = tm // SUBLANES
    nblk = T // SUBLANES
    tok = pl.BlockSpec((None, tm, D_MODEL), lambda b, i: (b, i, 0))
    prev = pl.BlockSpec((None, SUBLANES, D_MODEL), lambda b, i: (b, jnp.maximum(i * per - 1, 0), 0))
    nxt = pl.BlockSpec((None, SUBLANES, D_MODEL), lambda b, i: (b, jnp.minimum((i + 1) * per, nblk - 1), 0))
    return pl.pallas_call(
        functools.partial(_ffn_kernel, tm=tm, final=final),
        grid=(B, T // tm),
        in_specs=[
            tok, prev, nxt,
            _layer(gain, l),
            _mod_spec(l, 3, ctx_row), _mod_spec(l, 4, ctx_row), _mod_spec(l, 5, ctx_row),
            _layer(wu, l), _layer(cw, l), _layer(wd, l),
            _resident((1, D_MODEL)),
        ],
        out_specs=tok,
        out_shape=jax.ShapeDtypeStruct((B, T, D_MODEL), F32),
        scratch_shapes=[pltpu.VMEM((tm + 2 * SUBLANES, D_MODEL), BF16), pltpu.VMEM((tm, D_FF), BF16)],
        compiler_params=_params("arbitrary", "arbitrary"),
        name="ffn_final" if final else "ffn",
    )(x, x, x, gain, mod, mod, mod, wu, cw, wd, gain_final)


def _rope_tables(T):
    rows = T // GRID_W
    row = np.repeat(np.arange(rows), GRID_W).astype(np.float64)
    col = np.tile(np.arange(GRID_W), rows).astype(np.float64)
    n_freq = HEAD_DIM // 4
    inv_freq = ROPE_BASE ** (-np.arange(n_freq) / n_freq)
    ar, ac = row[:, None] * inv_freq[None, :], col[:, None] * inv_freq[None, :]
    cos = np.concatenate([np.cos(ar), np.cos(ar), np.cos(ac), np.cos(ac)], axis=1)
    sin = np.concatenate([-np.sin(ar), np.sin(ar), -np.sin(ac), np.sin(ac)], axis=1)
    reps = LANES // HEAD_DIM
    return jnp.asarray(np.tile(cos, (1, reps)), F32), jnp.asarray(np.tile(sin, (1, reps)), F32)


def kernel(x, c, ctx, c_ctx, w_mod, b_mod, norm_mix, norm_ffn, w_in, attn_sink, pool_w, pool_scale,
           w_br_attn, w_br_pool, w_br_four, w_out, w_up, conv_w, w_down, norm_final):
    B, T, _ = x.shape
    L = ctx.shape[1]
    depth = w_mod.shape[0]
    assert B < COND_ROWS and T % max(MERGE_TILE, FFN_TILE) == 0 and T % (ATTN_Q_BLOCKS * BLOCK) == 0
    assert L % BLOCK == 0

    cond = jnp.concatenate([c, c_ctx[None], jnp.zeros((COND_ROWS - B - 1, D_MODEL), F32)], axis=0)
    mod = _adaln(cond, w_mod, b_mod).reshape(depth, COND_ROWS, N_MOD, 1, D_MODEL)

    cos_t, sin_t = _rope_tables(T)
    cos_c = jnp.ones((B * L, LANES), F32)
    sin_c = jnp.zeros((B * L, LANES), F32)
    four_lat = _fourier_tables(T)
    four_ctx = _fourier_tables(L)

    w_a = w_in.astype(BF16)
    w_a_br = (w_br_attn.reshape(depth, N_KV_HEADS, Q_PER_KV, HEAD_DIM, D_MODEL)
              .transpose(0, 2, 1, 3, 4).reshape(depth, ATTN_WIDTH, D_MODEL).astype(BF16))
    w_p_br = w_br_pool.astype(BF16)
    w_f_br = w_br_four.astype(BF16)
    w_o = w_out.astype(BF16)
    groups = pool_w.shape[1]
    w_pool = jnp.einsum("lgab,gh->lgahb", pool_w, jnp.eye(groups, dtype=pool_w.dtype)).reshape(
        depth, POOL_WIDTH, POOL_WIDTH).astype(BF16)
    p_scale = pool_scale.reshape(depth, 1, POOL_WIDTH)
    w_u = w_up.astype(BF16)
    w_d = w_down.astype(BF16)
    g_mix = norm_mix.reshape(depth, 1, D_MODEL)
    g_ffn = norm_ffn.reshape(depth, 1, D_MODEL)
    gain_final = norm_final.reshape(1, D_MODEL)

    xc = ctx
    for l in range(depth):
        last = l == depth - 1
        k, v, q, u, f = _inproj(l, x, mod, g_mix, w_a, cos_t, sin_t, None)
        flat = lambda a: a.reshape((1, B * L) + a.shape[2:])
        kc, vc, qc, uc, fc = _inproj(l, flat(xc), mod, g_mix, w_a, cos_c, sin_c, B)
        kc, vc, qc, uc = (a.reshape((B, L) + a.shape[2:]) for a in (kc, vc, qc, uc))
        fc = fc.reshape(FOURIER_WIDTH // LANES, B, L, LANES).transpose(1, 0, 2, 3)
        o = _attention(l, q, k, v, kc, vc, attn_sink, True)
        x = _merge(l, x, o, _pool(l, u, w_pool, p_scale), _fourier(f, four_lat), mod, g_mix,
                   w_a, w_a_br, w_p_br, w_f_br, w_o, None)
        if not last:
            oc = _attention(l, qc, None, None, kc, vc, attn_sink, False)
            xc = _merge(l, flat(xc), flat(oc), flat(_pool(l, uc, w_pool, p_scale)), flat(_fourier(fc, four_ctx)),
                        mod, g_mix, w_a, w_a_br, w_p_br, w_f_br, w_o, B).reshape(B, L, D_MODEL)
        x = _ffn(l, x, mod, g_ffn, w_u, conv_w, w_d, gain_final, None, last)
        if not last:
            xc = _ffn(l, xc, mod, g_ffn, w_u, conv_w, w_d, gain_final, B, False)
    return x
```

```python
import functools
import math

import numpy as np
import jax
import jax.numpy as jnp
from jax import lax
from jax.experimental import pallas as pl
from jax.experimental.pallas import tpu as pltpu

D_MODEL = 1024
N_Q_HEADS = 8
N_KV_HEADS = 2
HEAD_DIM = 64
Q_PER_KV = N_Q_HEADS // N_KV_HEADS
ATTN_WIDTH = N_Q_HEADS * HEAD_DIM
KV_WIDTH = N_KV_HEADS * HEAD_DIM
BLOCK = 128
GRID_W = 64
ROPE_BASE = 10000.0
POOL_WIDTH = 256
POOL_GROUP_DIM = 64
FOURIER_WIDTH = 256
FOURIER_GROUP_DIM = 64
D_FF = 2816
N_MOD = 6
EPS = 1e-6
NEG_INF = -1e30
LOG2E = math.log2(math.e)

LANES = 128
SUBLANES = 8
VMEM_LIMIT_BYTES = 56 * 1024 * 1024

INPROJ_TILE = 1024
TOKEN_TILE = 512
MERGE_TILE = 1024
FFN_TILE = 1024
ATTN_Q_BLOCKS = 8
FF_CHUNK = 256
MERGE_CHUNK = 256
FOURIER_ROW_PAD = 8
FOURIER_UNROLL = 8
POOL_HALO = 8
ADALN_TILE = 1536
COND_ROWS = 8

F32 = jnp.float32
BF16 = jnp.bfloat16


def _dot(a, b):
    return jnp.dot(a, b, preferred_element_type=F32)


def _dot_nt(a, b):
    return lax.dot_general(a, b, (((1,), (1,)), ((), ())), preferred_element_type=F32)


def _modnorm(x, g, sc, sh):
    ms = jnp.mean(x * x, axis=-1, keepdims=True)
    y = x * lax.rsqrt(ms + EPS) * g
    return y * (1.0 + sc) + sh


def _resident(shape):
    nd = len(shape)
    return pl.BlockSpec(shape, lambda *_: (0,) * nd, pipeline_mode=pl.Buffered(1))


def _layer(arr, l):
    nd = arr.ndim - 1
    return pl.BlockSpec((None,) + arr.shape[1:], lambda *_: (l,) + (0,) * nd, pipeline_mode=pl.Buffered(1))


def _mod_spec(l, j, ctx_row):
    if ctx_row is None:
        return pl.BlockSpec((None, None, None, 1, D_MODEL), lambda b, i: (l, b, j, 0, 0))
    return pl.BlockSpec((None, None, None, 1, D_MODEL), lambda b, i: (l, ctx_row, j, 0, 0))


def _params(*sem):
    return pltpu.CompilerParams(dimension_semantics=sem, vmem_limit_bytes=VMEM_LIMIT_BYTES)


def _adaln_kernel(c_ref, w_ref, b_ref, o_ref):
    c = c_ref[...]
    a = (c * jax.nn.sigmoid(c)).astype(BF16)
    o_ref[...] = _dot(a, w_ref[...].astype(BF16)) + b_ref[...]


def _adaln(cond, w_mod, b_mod):
    depth, _, width = w_mod.shape
    return pl.pallas_call(
        _adaln_kernel,
        grid=(depth, width // ADALN_TILE),
        in_specs=[
            pl.BlockSpec((COND_ROWS, D_MODEL), lambda l, j: (0, 0)),
            pl.BlockSpec((None, D_MODEL, ADALN_TILE), lambda l, j: (l, 0, j)),
            pl.BlockSpec((None, 1, ADALN_TILE), lambda l, j: (l, 0, j)),
        ],
        out_specs=pl.BlockSpec((None, COND_ROWS, ADALN_TILE), lambda l, j: (l, 0, j)),
        out_shape=jax.ShapeDtypeStruct((depth, COND_ROWS, width), F32),
        compiler_params=_params("arbitrary", "arbitrary"),
        name="adaln",
    )(cond, w_mod, b_mod.reshape(depth, 1, width))


_K_COLS = 2 * KV_WIDTH
IN_TOKEN_COLS = 2 * KV_WIDTH + ATTN_WIDTH + POOL_WIDTH + FOURIER_WIDTH
_A_SPLITS = (0, KV_WIDTH, 2 * KV_WIDTH, 2 * KV_WIDTH + ATTN_WIDTH, 2 * KV_WIDTH + ATTN_WIDTH + POOL_WIDTH,
             IN_TOKEN_COLS)


def _inproj_kernel(x_ref, g_ref, sh_ref, sc_ref, w_ref, cos_ref, sin_ref,
                   k_ref, v_ref, q_ref, u_ref, f_ref):
    h = _modnorm(x_ref[...], g_ref[...], sc_ref[...], sh_ref[...]).astype(BF16)
    cos = cos_ref[...]
    sin = sin_ref[...]

    lane = lax.broadcasted_iota(jnp.int32, cos.shape, 1)
    first = (lane % (HEAD_DIM // 2)) < HEAD_DIM // 4
    low = lane < HEAD_DIM

    def rope(t):
        partner = jnp.where(first, pltpu.roll(t, LANES - HEAD_DIM // 4, 1), pltpu.roll(t, HEAD_DIM // 4, 1))
        return t * cos + partner * sin

    k0, v0, q0, u0, f0, end = _A_SPLITS
    tile = lambda a, j: a[:, j * LANES:(j + 1) * LANES]
    kvq = _dot(h, w_ref[:, k0:u0])
    k = rope(tile(kvq, 0))
    k_swapped = pltpu.roll(k, HEAD_DIM, 1)
    k_ref[:, 0:LANES] = jnp.where(low, k, k_swapped).astype(BF16)
    k_ref[:, LANES:2 * LANES] = jnp.where(low, k_swapped, k).astype(BF16)
    v_ref[...] = tile(kvq, v0 // LANES).astype(BF16)
    scale = HEAD_DIM ** -0.5 * LOG2E
    for j in range(ATTN_WIDTH // LANES):
        q_ref[:, j * LANES:(j + 1) * LANES] = (rope(tile(kvq, q0 // LANES + j)) * scale).astype(BF16)
    uf = _dot(h, w_ref[:, u0:end])
    u_ref[...] = uf[:, 0:f0 - u0]
    for j in range(FOURIER_WIDTH // LANES):
        f_ref[j] = tile(uf, (f0 - u0) // LANES + j)


def _inproj(l, x, mod, gain, w_a, cos_t, sin_t, ctx_row):
    B, T, _ = x.shape
    tm = min(INPROJ_TILE, T)
    tok = lambda w: pl.BlockSpec((None, tm, w), lambda b, i: (b, i, 0))
    return pl.pallas_call(
        _inproj_kernel,
        grid=(B, T // tm),
        in_specs=[
            tok(D_MODEL),
            _layer(gain, l),
            _mod_spec(l, 0, ctx_row), _mod_spec(l, 1, ctx_row),
            pl.BlockSpec((None, D_MODEL, IN_TOKEN_COLS), lambda b, i: (l, 0, 0), pipeline_mode=pl.Buffered(1)),
            pl.BlockSpec((tm, LANES), lambda b, i: (i, 0)),
            pl.BlockSpec((tm, LANES), lambda b, i: (i, 0)),
        ],
        out_specs=[tok(_K_COLS), tok(KV_WIDTH), tok(ATTN_WIDTH), tok(POOL_WIDTH),
                   pl.BlockSpec((None, FOURIER_WIDTH // LANES, tm, LANES), lambda b, i: (b, 0, i, 0))],
        out_shape=[
            jax.ShapeDtypeStruct((B, T, _K_COLS), BF16),
            jax.ShapeDtypeStruct((B, T, KV_WIDTH), BF16),
            jax.ShapeDtypeStruct((B, T, ATTN_WIDTH), BF16),
            jax.ShapeDtypeStruct((B, T, POOL_WIDTH), F32),
            jax.ShapeDtypeStruct((B, FOURIER_WIDTH // LANES, T, LANES), F32),
        ],
        compiler_params=_params("arbitrary", "arbitrary"),
        name="inproj",
    )(x, gain, mod, mod, w_a, cos_t, sin_t)


def _attn_kernel(sink_ref, q_ref, *refs, layer, steps, band):
    if band:
        kp_ref, km_ref, kn_ref, vp_ref, vm_ref, vn_ref, kc_ref, vc_ref, o_ref = refs
    else:
        kc_ref, vc_ref, o_ref = refs
    n = pl.program_id(1)
    rows = Q_PER_KV * BLOCK
    lane = lax.broadcasted_iota(jnp.int32, (BLOCK, LANES), 1)
    low = lane < HEAD_DIM
    rgrp = lax.broadcasted_iota(jnp.int32, (rows, 1), 0) // BLOCK
    if band:
        qi = lax.broadcasted_iota(jnp.int32, (rows, BLOCK), 0) % BLOCK
        kj = lax.broadcasted_iota(jnp.int32, (rows, BLOCK), 1)
        in_prev = kj >= qi
        in_next = kj <= qi
    blk = lambda ref, j, cols: ref[j * BLOCK:(j + 1) * BLOCK, cols]
    every = slice(None)

    def one_head(q0, h, k_parts, v_all, ok_prev, ok_next):
        qs = []
        for t in range(Q_PER_KV // 2):
            c0 = (h * (Q_PER_KV // 2) + t) * LANES
            qt = q_ref[q0:q0 + BLOCK, c0:c0 + LANES]
            qs.append(jnp.where(low, qt, jnp.zeros_like(qt)))
            qs.append(jnp.where(low, jnp.zeros_like(qt), qt))
        qh = jnp.concatenate(qs, axis=0)
        s = _dot_nt(qh, jnp.concatenate(k_parts, axis=0))
        tiles = [s[:, j * LANES:(j + 1) * LANES] for j in range(s.shape[1] // LANES)]
        if band:
            tiles[0] = jnp.where(ok_prev, tiles[0], NEG_INF)
            tiles[2] = jnp.where(ok_next, tiles[2], NEG_INF)
        sink = jnp.zeros((rows, 1), F32)
        for g in range(Q_PER_KV):
            sink = jnp.where(rgrp == g, sink_ref[layer, h * Q_PER_KV + g] * LOG2E, sink)
        m_t = tiles[0]
        for t in tiles[1:]:
            m_t = jnp.maximum(m_t, t)
        m = jnp.maximum(jnp.max(m_t, axis=-1, keepdims=True), sink)
        p = jnp.concatenate([jnp.exp2((t - m).astype(BF16)) for t in tiles], axis=1)
        return _dot(p, v_all), jnp.exp2(sink - m)

    n_sub = ATTN_Q_BLOCKS if band else 1
    for sub in range(n_sub):
        if band:
            pick = lambda pm, mid, nx, j, cols: (blk(pm, 0, cols) if j < 0 else
                                                 blk(nx, 0, cols) if j >= n_sub else blk(mid, j, cols))
            ok_prev = jnp.logical_and(in_prev, n > 0) if sub == 0 else in_prev
            ok_next = jnp.logical_and(in_next, n < steps - 1) if sub == n_sub - 1 else in_next
            v_all = jnp.concatenate([pick(vp_ref, vm_ref, vn_ref, sub + d, every) for d in (-1, 0, 1)]
                                    + [vc_ref[...]], axis=0)
        else:
            ok_prev = ok_next = None
            v_all = vc_ref[...]
        v_low = lax.broadcasted_iota(jnp.int32, v_all.shape, 1) < HEAD_DIM
        ones = jnp.ones_like(v_all)
        v_aug = [jnp.where(v_low, v_all, ones), jnp.where(v_low, ones, v_all)]
        outs = []
        for h in range(N_KV_HEADS):
            ks = slice(h * LANES, (h + 1) * LANES)
            k_parts = [kc_ref[:, ks]]
            if band:
                k_parts = [pick(kp_ref, km_ref, kn_ref, sub + d, ks) for d in (-1, 0, 1)] + k_parts
            outs.append(one_head(sub * BLOCK, h, k_parts, v_aug[h], ok_prev, ok_next))
        (o0, e0), (o1, e1) = outs
        for g in range(Q_PER_KV):
            r = slice(g * BLOCK, (g + 1) * BLOCK)
            num = jnp.where(low, o0[r], o1[r])
            den = pltpu.roll(jnp.where(low, o1[r], o0[r]), HEAD_DIM, 1) + jnp.where(low, e0[r], e1[r])
            o_ref[sub * BLOCK:(sub + 1) * BLOCK, g * LANES:(g + 1) * LANES] = (num / den).astype(BF16)


def _attention(l, q, k, v, kc, vc, sink, band):
    B, T, _ = q.shape
    L = kc.shape[1]
    nq = ATTN_Q_BLOCKS if band else 1
    rows = nq * BLOCK
    steps = T // rows
    nb = T // BLOCK
    qspec = pl.BlockSpec((None, rows, ATTN_WIDTH), lambda b, n: (b, n, 0))
    kprev = lambda w: pl.BlockSpec((None, BLOCK, w), lambda b, n: (b, jnp.maximum(n * nq - 1, 0), 0))
    kmid = lambda w: pl.BlockSpec((None, rows, w), lambda b, n: (b, n, 0))
    knext = lambda w: pl.BlockSpec((None, BLOCK, w), lambda b, n: (b, jnp.minimum((n + 1) * nq, nb - 1), 0))
    cspec = lambda w: pl.BlockSpec((None, L, w), lambda b, n: (b, 0, 0))
    in_specs = [pl.BlockSpec(memory_space=pltpu.SMEM), qspec]
    args = [sink, q]
    if band:
        in_specs += [kprev(_K_COLS), kmid(_K_COLS), knext(_K_COLS),
                     kprev(KV_WIDTH), kmid(KV_WIDTH), knext(KV_WIDTH)]
        args += [k, k, k, v, v, v]
    in_specs += [cspec(_K_COLS), cspec(KV_WIDTH)]
    args += [kc, vc]
    return pl.pallas_call(
        functools.partial(_attn_kernel, layer=l, steps=steps, band=band),
        grid=(B, steps),
        in_specs=in_specs,
        out_specs=pl.BlockSpec((None, rows, ATTN_WIDTH), lambda b, n: (b, n, 0)),
        out_shape=jax.ShapeDtypeStruct((B, T, ATTN_WIDTH), BF16),
        compiler_params=_params("arbitrary", "arbitrary"),
        name="attention_band" if band else "attention_ctx",
    )(*args)


def _pool_kernel(u_ref, w_ref, s_ref, o_ref, pad_ref, *, T, tc):
    zeros = jnp.zeros((POOL_HALO, POOL_WIDTH), F32)
    pad_ref[0:POOL_HALO, :] = zeros
    pad_ref[POOL_HALO + T:POOL_HALO + T + POOL_HALO, :] = zeros
    pad_ref[POOL_HALO:POOL_HALO + T, :] = u_ref[...]
    lane = lax.broadcasted_iota(jnp.int32, (tc, LANES), 1)
    upper = lane >= POOL_GROUP_DIM
    row = lax.broadcasted_iota(jnp.int32, (tc, LANES), 0)
    for c in range(T // tc):
        base = POOL_HALO + c * tc
        t = row + c * tc
        pooled = []
        for tile in range(POOL_WIDTH // LANES):
            w_small = 2 << (2 * tile)
            w_big = 2 * w_small
            cols = slice(tile * LANES, (tile + 1) * LANES)
            load = lambda d: pad_ref[base + d:base + d + tc, cols]
            inner = load(-(w_small // 2))
            for d in range(-(w_small // 2) + 1, w_small // 2):
                inner = inner + load(d)
            outer = load(-(w_big // 2))
            for d in list(range(-(w_big // 2) + 1, -(w_small // 2))) + list(range(w_small // 2, w_big // 2)):
                outer = outer + load(d)
            win = inner + jnp.where(upper, outer, 0.0)
            half = jnp.where(upper, w_big // 2, w_small // 2)
            lo = jnp.maximum(t - half, 0)
            hi = jnp.minimum(t + half, T)
            cnt = (hi - lo).astype(F32)
            pooled.append(win / cnt - load(0))
        p = jnp.concatenate(pooled, axis=1).astype(BF16)
        y = _dot(p, w_ref[...]) * s_ref[...]
        o_ref[c * tc:(c + 1) * tc, :] = y.astype(BF16)


def _pool(l, u, w_bd, scale):
    B, T, _ = u.shape
    tc = min(TOKEN_TILE, T)
    return pl.pallas_call(
        functools.partial(_pool_kernel, T=T, tc=tc),
        grid=(B,),
        in_specs=[
            pl.BlockSpec((None, T, POOL_WIDTH), lambda b: (b, 0, 0)),
            _layer(w_bd, l),
            _layer(scale, l),
        ],
        out_specs=pl.BlockSpec((None, T, POOL_WIDTH), lambda b: (b, 0, 0)),
        out_shape=jax.ShapeDtypeStruct((B, T, POOL_WIDTH), BF16),
        scratch_shapes=[pltpu.VMEM((T + 2 * POOL_HALO, POOL_WIDTH), F32)],
        compiler_params=_params("arbitrary"),
        name="pool",
    )(u, w_bd, scale)


def _fourier_factors(T):
    n1 = 1 << (int(math.log2(T)) // 2)
    return n1, T // n1


def _split_const(a):
    a = jnp.asarray(a, F32)
    hi = a.astype(BF16)
    return hi, (a - hi.astype(F32)).astype(BF16)


def _fourier_tables(T):
    n1, n2 = _fourier_factors(T)
    gd = FOURIER_GROUP_DIM
    t2 = np.arange(n2)[:, None, None]
    k1 = np.arange(n1)[None, :, None]
    t1 = np.arange(n1)[None, None, :]
    ph = -2.0 * np.pi * (t2 * k1 / T + t1 * k1 / n1)
    m1 = np.concatenate([np.cos(ph), np.sin(ph)], axis=1) / math.sqrt(n1)
    a3 = 2.0 * np.pi * np.outer(np.arange(n2), np.arange(n2)) / n2
    c3, s3 = np.cos(a3) / math.sqrt(n2), np.sin(a3) / math.sqrt(n2)
    m3 = np.block([[c3, s3], [-s3, c3]])
    c = np.arange(gd)
    ang = 2.0 * np.pi * np.outer(c, c) / gd
    eye = np.eye(FOURIER_WIDTH // gd)
    cd = np.concatenate([np.kron(eye, np.cos(ang)), np.kron(eye, np.sin(ang))], axis=0) / math.sqrt(gd)
    return _split_const(m1) + _split_const(m3) + _split_const(cd)


def _split(a):
    hi = a.astype(BF16)
    return hi, (a - hi.astype(F32)).astype(BF16)


def _dot3(a, b):
    return _dot(a[0], b[0]) + (_dot(a[1], b[0]) + _dot(a[0], b[1]))


def _fourier_kernel(f_ref, m1h_ref, m1l_ref, m3h_ref, m3l_ref, cdh_ref, cdl_ref, o_ref, a_ref, g_ref,
                    *, n1, n2, tc, unroll):
    T = n1 * n2
    wt = FOURIER_WIDTH // LANES
    lanes = lambda j: slice(j * LANES, (j + 1) * LANES)
    pitch = n1 + FOURIER_ROW_PAD

    def over_t1(t2, carry):
        rows = pl.ds(t2, n1, stride=n2)
        x = jnp.concatenate([f_ref[j, rows, :] for j in range(wt)], axis=1)
        a = _dot3((m1h_ref[t2], m1l_ref[t2]), _split(x))
        off = pl.multiple_of(t2 * pitch, SUBLANES)
        for j in range(wt):
            a_ref[j, pl.ds(off, n1), :] = a[:n1, lanes(j)]
            a_ref[wt + j, pl.ds(off, n1), :] = a[n1:, lanes(j)]
        return carry

    lax.fori_loop(0, n2, over_t1, 0, unroll=unroll)

    def over_t2(k1, carry):
        rows = pl.ds(k1, n2, stride=pitch)
        b = jnp.concatenate(
            [jnp.concatenate([a_ref[h * wt + j, rows, :] for j in range(wt)], axis=1) for h in range(2)], axis=0)
        g = _dot3((m3h_ref[...], m3l_ref[...]), _split(b))
        for j in range(wt):
            g_ref[j, rows, :] = g[:n2, lanes(j)]
            g_ref[wt + j, rows, :] = g[n2:, lanes(j)]
        return carry

    lax.fori_loop(0, n1, over_t2, 0, unroll=unroll)

    per = tc // n1
    for c in range(T // tc):
        groups = [slice((c * per + i) * pitch, (c * per + i) * pitch + n1) for i in range(per)]
        g = jnp.concatenate([jnp.concatenate([g_ref[j, r, :] for r in groups], axis=0) for j in range(2 * wt)],
                            axis=1)
        o_ref[c * tc:(c + 1) * tc, :] = _dot3(_split(g), (cdh_ref[...], cdl_ref[...])).astype(BF16)


def _fourier(f, tables):
    B, wt, T, _ = f.shape
    n1, n2 = _fourier_factors(T)
    return pl.pallas_call(
        functools.partial(_fourier_kernel, n1=n1, n2=n2, tc=min(TOKEN_TILE, T), unroll=FOURIER_UNROLL),
        grid=(B,),
        in_specs=[pl.BlockSpec((None, wt, T, LANES), lambda b: (b, 0, 0, 0))] + [_resident(t.shape) for t in tables],
        out_specs=pl.BlockSpec((None, T, FOURIER_WIDTH), lambda b: (b, 0, 0)),
        out_shape=jax.ShapeDtypeStruct((B, T, FOURIER_WIDTH), BF16),
        scratch_shapes=[pltpu.VMEM((2 * wt, n2 * (n1 + FOURIER_ROW_PAD), LANES), F32)] * 2,
        compiler_params=_params("arbitrary"),
        name="fourier",
    )(f, *tables)


def _merge_kernel(x_ref, o_ref, p_ref, f_ref, g_ref, sh_ref, sc_ref, gt_ref,
                  wg_ref, wa_ref, wp_ref, wf_ref, wo_ref, out_ref, y_ref):
    x = x_ref[...]
    h = _modnorm(x, g_ref[...], sc_ref[...], sh_ref[...]).astype(BF16)
    o = o_ref[...]
    p = p_ref[...]
    f = f_ref[...]
    for j in range(D_MODEL // MERGE_CHUNK):
        c = slice(j * MERGE_CHUNK, (j + 1) * MERGE_CHUNK)
        gcol = lambda b: IN_TOKEN_COLS + b * D_MODEL + j * MERGE_CHUNK
        gate = lambda b: jax.nn.sigmoid(_dot(h, wg_ref[:, gcol(b):gcol(b) + MERGE_CHUNK]))
        y = (gate(0) * _dot(o, wa_ref[:, c]) + gate(1) * _dot(p, wp_ref[:, c])
             + gate(2) * _dot(f, wf_ref[:, c]))
        y_ref[:, c] = y.astype(BF16)
    out_ref[...] = x + gt_ref[...] * _dot(y_ref[...], wo_ref[...])


def _merge(l, x, o, p, f, mod, gain, wg, wa, wp, wf, wo, ctx_row):
    B, T, _ = x.shape
    tm = min(MERGE_TILE, T)
    tok = lambda w: pl.BlockSpec((None, tm, w), lambda b, i: (b, i, 0))
    return pl.pallas_call(
        _merge_kernel,
        grid=(B, T // tm),
        in_specs=[
            tok(D_MODEL), tok(ATTN_WIDTH), tok(POOL_WIDTH), tok(FOURIER_WIDTH),
            _layer(gain, l),
            _mod_spec(l, 0, ctx_row), _mod_spec(l, 1, ctx_row), _mod_spec(l, 2, ctx_row),
            _layer(wg, l), _layer(wa, l), _layer(wp, l), _layer(wf, l), _layer(wo, l),
        ],
        out_specs=tok(D_MODEL),
        out_shape=jax.ShapeDtypeStruct((B, T, D_MODEL), F32),
        scratch_shapes=[pltpu.VMEM((tm, D_MODEL), BF16)],
        compiler_params=_params("arbitrary", "arbitrary"),
        name="merge",
    )(x, o, p, f, gain, mod, mod, mod, wg, wa, wp, wf, wo)


def _ffn_kernel(x_ref, xp_ref, xn_ref, g_ref, sh_ref, sc_ref, gt_ref, wu_ref, cw_ref, wd_ref, gf_ref,
                out_ref, h_ref, a_ref, *, tm, seg, final):
    i = pl.program_id(1)
    last = pl.num_programs(1) - 1
    g, sh, sc = g_ref[...], sh_ref[...], sc_ref[...]
    x = x_ref[...]
    H = SUBLANES
    h_ref[H:H + tm, :] = _modnorm(x, g, sc, sh).astype(BF16)
    keep_prev = (i > 0).astype(F32)
    keep_next = (i < last).astype(F32)
    h_ref[0:H, :] = (_modnorm(xp_ref[...], g, sc, sh) * keep_prev).astype(BF16)
    h_ref[H + tm:H + tm + H, :] = (_modnorm(xn_ref[...], g, sc, sh) * keep_next).astype(BF16)
    hf = h_ref[...]
    rows = tm + 2 * H

    if seg is not None:
        pos = (lax.broadcasted_iota(jnp.int32, (rows, 1), 0) - H) % seg
        tap_prev = (pos != 0).astype(F32)
        tap_next = (pos != seg - 1).astype(F32)

    def conv(up, cols):
        w = cw_ref[:, cols]
        before, after = pltpu.roll(up, 1, 0), pltpu.roll(up, rows - 1, 0)
        if seg is not None:
            before, after = before * tap_prev, after * tap_next
        y = before * w[0:1] + up * w[1:2] + after * w[2:3]
        return y[H:H + tm]

    for j in range(D_FF // FF_CHUNK):
        cv = slice(j * FF_CHUNK, (j + 1) * FF_CHUNK)
        cg = slice(D_FF + j * FF_CHUNK, D_FF + (j + 1) * FF_CHUNK)
        val = conv(_dot(hf, wu_ref[:, cv]), cv)
        gate = conv(_dot(hf, wu_ref[:, cg]), cg)
        a_ref[:, cv] = (val * (gate * jax.nn.sigmoid(gate))).astype(BF16)
    y = x + gt_ref[...] * _dot(a_ref[...], wd_ref[...])
    if final:
        ms = jnp.mean(y * y, axis=-1, keepdims=True)
        y = y * lax.rsqrt(ms + EPS) * gf_ref[...]
    out_ref[...] = y


def _ffn(l, x, mod, gain, wu, cw, wd, gain_final, ctx_row, final, seg=None):
    B, T, _ = x.shape
    tm = min(FFN_TILE, T)
    assert seg is None or (tm == T and T % seg == 0)
    per = tm // SUBLANES
    nblk = T // SUBLANES
    tok = pl.BlockSpec((None, tm, D_MODEL), lambda b, i: (b, i, 0))
    prev = pl.BlockSpec((None, SUBLANES, D_MODEL), lambda b, i: (b, jnp.maximum(i * per - 1, 0), 0))
    nxt = pl.BlockSpec((None, SUBLANES, D_MODEL), lambda b, i: (b, jnp.minimum((i + 1) * per, nblk - 1), 0))
    return pl.pallas_call(
        functools.partial(_ffn_kernel, tm=tm, seg=seg, final=final),
        grid=(B, T // tm),
        in_specs=[
            tok, prev, nxt,
            _layer(gain, l),
            _mod_spec(l, 3, ctx_row), _mod_spec(l, 4, ctx_row), _mod_spec(l, 5, ctx_row),
            _layer(wu, l), _layer(cw, l), _layer(wd, l),
            _resident((1, D_MODEL)),
        ],
        out_specs=tok,
        out_shape=jax.ShapeDtypeStruct((B, T, D_MODEL), F32),
        scratch_shapes=[pltpu.VMEM((tm + 2 * SUBLANES, D_MODEL), BF16), pltpu.VMEM((tm, D_FF), BF16)],
        compiler_params=_params("arbitrary", "arbitrary"),
        name="ffn_final" if final else "ffn",
    )(x, x, x, gain, mod, mod, mod, wu, cw, wd, gain_final)


def _rope_tables(T):
    rows = T // GRID_W
    row = np.repeat(np.arange(rows), GRID_W).astype(np.float64)
    col = np.tile(np.arange(GRID_W), rows).astype(np.float64)
    n_freq = HEAD_DIM // 4
    inv_freq = ROPE_BASE ** (-np.arange(n_freq) / n_freq)
    ar, ac = row[:, None] * inv_freq[None, :], col[:, None] * inv_freq[None, :]
    cos = np.concatenate([np.cos(ar), np.cos(ar), np.cos(ac), np.cos(ac)], axis=1)
    sin = np.concatenate([-np.sin(ar), np.sin(ar), -np.sin(ac), np.sin(ac)], axis=1)
    reps = LANES // HEAD_DIM
    return jnp.asarray(np.tile(cos, (1, reps)), F32), jnp.asarray(np.tile(sin, (1, reps)), F32)


def kernel(x, c, ctx, c_ctx, w_mod, b_mod, norm_mix, norm_ffn, w_in, attn_sink, pool_w, pool_scale,
           w_br_attn, w_br_pool, w_br_four, w_out, w_up, conv_w, w_down, norm_final):
    B, T, _ = x.shape
    L = ctx.shape[1]
    depth = w_mod.shape[0]
    assert B < COND_ROWS and T % max(INPROJ_TILE, MERGE_TILE, FFN_TILE) == 0 and T % (ATTN_Q_BLOCKS * BLOCK) == 0
    assert L % BLOCK == 0

    cond = jnp.concatenate([c, c_ctx[None], jnp.zeros((COND_ROWS - B - 1, D_MODEL), F32)], axis=0)
    mod = _adaln(cond, w_mod, b_mod).reshape(depth, COND_ROWS, N_MOD, 1, D_MODEL)

    cos_t, sin_t = _rope_tables(T)
    cos_c = jnp.ones((B * L, LANES), F32)
    sin_c = jnp.zeros((B * L, LANES), F32)
    four_lat = _fourier_tables(T)
    four_ctx = _fourier_tables(L)

    w_a = w_in.astype(BF16)
    w_a_br = (w_br_attn.reshape(depth, N_KV_HEADS, Q_PER_KV, HEAD_DIM, D_MODEL)
              .transpose(0, 2, 1, 3, 4).reshape(depth, ATTN_WIDTH, D_MODEL).astype(BF16))
    w_p_br = w_br_pool.astype(BF16)
    w_f_br = w_br_four.astype(BF16)
    w_o = w_out.astype(BF16)
    groups = pool_w.shape[1]
    w_pool = jnp.einsum("lgab,gh->lgahb", pool_w, jnp.eye(groups, dtype=pool_w.dtype)).reshape(
        depth, POOL_WIDTH, POOL_WIDTH).astype(BF16)
    p_scale = pool_scale.reshape(depth, 1, POOL_WIDTH)
    w_u = w_up.astype(BF16)
    w_d = w_down.astype(BF16)
    g_mix = norm_mix.reshape(depth, 1, D_MODEL)
    g_ffn = norm_ffn.reshape(depth, 1, D_MODEL)
    gain_final = norm_final.reshape(1, D_MODEL)

    xc = ctx
    for l in range(depth):
        last = l == depth - 1
        k, v, q, u, f = _inproj(l, x, mod, g_mix, w_a, cos_t, sin_t, None)
        flat = lambda a: a.reshape((1, B * L) + a.shape[2:])
        kc, vc, qc, uc, fc = _inproj(l, flat(xc), mod, g_mix, w_a, cos_c, sin_c, B)
        kc, vc, qc, uc = (a.reshape((B, L) + a.shape[2:]) for a in (kc, vc, qc, uc))
        fc = fc.reshape(FOURIER_WIDTH // LANES, B, L, LANES).transpose(1, 0, 2, 3)
        o = _attention(l, q, k, v, kc, vc, attn_sink, True)
        x = _merge(l, x, o, _pool(l, u, w_pool, p_scale), _fourier(f, four_lat), mod, g_mix,
                   w_a, w_a_br, w_p_br, w_f_br, w_o, None)
        if not last:
            oc = _attention(l, qc, None, None, kc, vc, attn_sink, False)
            xc = _merge(l, flat(xc), flat(oc), flat(_pool(l, uc, w_pool, p_scale)), flat(_fourier(fc, four_ctx)),
                        mod, g_mix, w_a, w_a_br, w_p_br, w_f_br, w_o, B).reshape(B, L, D_MODEL)
        x = _ffn(l, x, mod, g_ffn, w_u, conv_w, w_d, gain_final, None, last)
        if not last:
            xc = _ffn(l, flat(xc), mod, g_ffn, w_u, conv_w, w_d, gain_final, B, False, seg=L).reshape(B, L, D_MODEL)
    return x
```

```python
import functools
import math

import numpy as np
import jax
import jax.numpy as jnp
from jax import lax
from jax.experimental import pallas as pl
from jax.experimental.pallas import tpu as pltpu

D_MODEL = 1024
N_Q_HEADS = 8
N_KV_HEADS = 2
HEAD_DIM = 64
Q_PER_KV = N_Q_HEADS // N_KV_HEADS
ATTN_WIDTH = N_Q_HEADS * HEAD_DIM
KV_WIDTH = N_KV_HEADS * HEAD_DIM
BLOCK = 128
GRID_W = 64
ROPE_BASE = 10000.0
POOL_WIDTH = 256
POOL_GROUP_DIM = 64
FOURIER_WIDTH = 256
FOURIER_GROUP_DIM = 64
D_FF = 2816
N_MOD = 6
EPS = 1e-6
NEG_INF = -1e30
LOG2E = math.log2(math.e)

LANES = 128
SUBLANES = 8
VMEM_LIMIT_BYTES = 56 * 1024 * 1024

INPROJ_TILE = 1024
TOKEN_TILE = 512
MERGE_TILE = 1024
FFN_TILE = 1024
ATTN_Q_BLOCKS = 8
ATTN_GROUP = 2
FF_CHUNK = 256
MERGE_CHUNK = 256
FOURIER_ROW_PAD = 8
FOURIER_UNROLL = 8
POOL_HALO = 8
ADALN_TILE = 1536
COND_ROWS = 8

F32 = jnp.float32
BF16 = jnp.bfloat16


def _dot(a, b):
    return jnp.dot(a, b, preferred_element_type=F32)


def _dot_nt(a, b):
    return lax.dot_general(a, b, (((1,), (1,)), ((), ())), preferred_element_type=F32)


def _modnorm(x, g, sc, sh):
    ms = jnp.mean(x * x, axis=-1, keepdims=True)
    y = x * lax.rsqrt(ms + EPS) * g
    return y * (1.0 + sc) + sh


def _resident(shape):
    nd = len(shape)
    return pl.BlockSpec(shape, lambda *_: (0,) * nd, pipeline_mode=pl.Buffered(1))


def _layer(arr, l):
    nd = arr.ndim - 1
    return pl.BlockSpec((None,) + arr.shape[1:], lambda *_: (l,) + (0,) * nd, pipeline_mode=pl.Buffered(1))


def _mod_spec(l, j, ctx_row):
    if ctx_row is None:
        return pl.BlockSpec((None, None, None, 1, D_MODEL), lambda b, i: (l, b, j, 0, 0))
    return pl.BlockSpec((None, None, None, 1, D_MODEL), lambda b, i: (l, ctx_row, j, 0, 0))


def _params(*sem):
    return pltpu.CompilerParams(dimension_semantics=sem, vmem_limit_bytes=VMEM_LIMIT_BYTES)


def _adaln_kernel(c_ref, w_ref, b_ref, o_ref):
    c = c_ref[...]
    a = (c * jax.nn.sigmoid(c)).astype(BF16)
    o_ref[...] = _dot(a, w_ref[...].astype(BF16)) + b_ref[...]


def _adaln(cond, w_mod, b_mod):
    depth, _, width = w_mod.shape
    return pl.pallas_call(
        _adaln_kernel,
        grid=(depth, width // ADALN_TILE),
        in_specs=[
            pl.BlockSpec((COND_ROWS, D_MODEL), lambda l, j: (0, 0)),
            pl.BlockSpec((None, D_MODEL, ADALN_TILE), lambda l, j: (l, 0, j)),
            pl.BlockSpec((None, 1, ADALN_TILE), lambda l, j: (l, 0, j)),
        ],
        out_specs=pl.BlockSpec((None, COND_ROWS, ADALN_TILE), lambda l, j: (l, 0, j)),
        out_shape=jax.ShapeDtypeStruct((depth, COND_ROWS, width), F32),
        compiler_params=_params("arbitrary", "arbitrary"),
        name="adaln",
    )(cond, w_mod, b_mod.reshape(depth, 1, width))


_K_COLS = 2 * KV_WIDTH
IN_TOKEN_COLS = 2 * KV_WIDTH + ATTN_WIDTH + POOL_WIDTH + FOURIER_WIDTH
_A_SPLITS = (0, KV_WIDTH, 2 * KV_WIDTH, 2 * KV_WIDTH + ATTN_WIDTH, 2 * KV_WIDTH + ATTN_WIDTH + POOL_WIDTH,
             IN_TOKEN_COLS)


def _inproj_kernel(x_ref, g_ref, sh_ref, sc_ref, w_ref, cos_ref, sin_ref,
                   k_ref, v_ref, q_ref, u_ref, f_ref):
    h = _modnorm(x_ref[...], g_ref[...], sc_ref[...], sh_ref[...]).astype(BF16)
    cos = cos_ref[...]
    sin = sin_ref[...]

    lane = lax.broadcasted_iota(jnp.int32, cos.shape, 1)
    first = (lane % (HEAD_DIM // 2)) < HEAD_DIM // 4
    low = lane < HEAD_DIM

    def rope(t):
        partner = jnp.where(first, pltpu.roll(t, LANES - HEAD_DIM // 4, 1), pltpu.roll(t, HEAD_DIM // 4, 1))
        return t * cos + partner * sin

    k0, v0, q0, u0, f0, end = _A_SPLITS
    tile = lambda a, j: a[:, j * LANES:(j + 1) * LANES]
    kvq = _dot(h, w_ref[:, k0:u0])
    k = rope(tile(kvq, 0))
    k_swapped = pltpu.roll(k, HEAD_DIM, 1)
    k_ref[:, 0:LANES] = jnp.where(low, k, k_swapped).astype(BF16)
    k_ref[:, LANES:2 * LANES] = jnp.where(low, k_swapped, k).astype(BF16)
    v_ref[...] = tile(kvq, v0 // LANES).astype(BF16)
    scale = HEAD_DIM ** -0.5 * LOG2E
    for j in range(ATTN_WIDTH // LANES):
        q_ref[:, j * LANES:(j + 1) * LANES] = (rope(tile(kvq, q0 // LANES + j)) * scale).astype(BF16)
    uf = _dot(h, w_ref[:, u0:end])
    u_ref[...] = uf[:, 0:f0 - u0]
    for j in range(FOURIER_WIDTH // LANES):
        f_ref[j] = tile(uf, (f0 - u0) // LANES + j)


def _inproj(l, x, mod, gain, w_a, cos_t, sin_t, ctx_row):
    B, T, _ = x.shape
    tm = min(INPROJ_TILE, T)
    tok = lambda w: pl.BlockSpec((None, tm, w), lambda b, i: (b, i, 0))
    return pl.pallas_call(
        _inproj_kernel,
        grid=(B, T // tm),
        in_specs=[
            tok(D_MODEL),
            _layer(gain, l),
            _mod_spec(l, 0, ctx_row), _mod_spec(l, 1, ctx_row),
            pl.BlockSpec((None, D_MODEL, IN_TOKEN_COLS), lambda b, i: (l, 0, 0), pipeline_mode=pl.Buffered(1)),
            pl.BlockSpec((tm, LANES), lambda b, i: (i, 0)),
            pl.BlockSpec((tm, LANES), lambda b, i: (i, 0)),
        ],
        out_specs=[tok(_K_COLS), tok(KV_WIDTH), tok(ATTN_WIDTH), tok(POOL_WIDTH),
                   pl.BlockSpec((None, FOURIER_WIDTH // LANES, tm, LANES), lambda b, i: (b, 0, i, 0))],
        out_shape=[
            jax.ShapeDtypeStruct((B, T, _K_COLS), BF16),
            jax.ShapeDtypeStruct((B, T, KV_WIDTH), BF16),
            jax.ShapeDtypeStruct((B, T, ATTN_WIDTH), BF16),
            jax.ShapeDtypeStruct((B, T, POOL_WIDTH), F32),
            jax.ShapeDtypeStruct((B, FOURIER_WIDTH // LANES, T, LANES), F32),
        ],
        compiler_params=_params("arbitrary", "arbitrary"),
        name="inproj",
    )(x, gain, mod, mod, w_a, cos_t, sin_t)


def _attn_kernel(sink_ref, q_ref, *refs, layer, steps, band):
    if band:
        kp_ref, km_ref, kn_ref, vp_ref, vm_ref, vn_ref, kc_ref, vc_ref, o_ref = refs
    else:
        kc_ref, vc_ref, o_ref = refs
    n = pl.program_id(1)
    rows = Q_PER_KV * BLOCK
    lane = lax.broadcasted_iota(jnp.int32, (BLOCK, LANES), 1)
    low = lane < HEAD_DIM
    rgrp = lax.broadcasted_iota(jnp.int32, (rows, 1), 0) // BLOCK
    if band:
        qi = lax.broadcasted_iota(jnp.int32, (rows, BLOCK), 0) % BLOCK
        kj = lax.broadcasted_iota(jnp.int32, (rows, BLOCK), 1)
        in_prev = kj >= qi
        in_next = kj <= qi
    blk = lambda ref, j, cols: ref[j * BLOCK:(j + 1) * BLOCK, cols]
    every = slice(None)

    def scores(q0, h, k_parts, ok_prev, ok_next):
        qs = []
        for t in range(Q_PER_KV // 2):
            c0 = (h * (Q_PER_KV // 2) + t) * LANES
            qt = q_ref[q0:q0 + BLOCK, c0:c0 + LANES]
            qs.append(jnp.where(low, qt, jnp.zeros_like(qt)))
            qs.append(jnp.where(low, jnp.zeros_like(qt), qt))
        qh = jnp.concatenate(qs, axis=0)
        s = _dot_nt(qh, jnp.concatenate(k_parts, axis=0))
        tiles = [s[:, j * LANES:(j + 1) * LANES] for j in range(s.shape[1] // LANES)]
        if band:
            tiles[0] = jnp.where(ok_prev, tiles[0], NEG_INF)
            tiles[2] = jnp.where(ok_next, tiles[2], NEG_INF)
        sink = jnp.zeros((rows, 1), F32)
        for g in range(Q_PER_KV):
            sink = jnp.where(rgrp == g, sink_ref[layer, h * Q_PER_KV + g] * LOG2E, sink)
        m_t = tiles[0]
        for t in tiles[1:]:
            m_t = jnp.maximum(m_t, t)
        m = jnp.maximum(jnp.max(m_t, axis=-1, keepdims=True), sink)
        return tiles, m, sink

    def weights(tiles, m, sink, v_all):
        p = jnp.concatenate([jnp.exp2((t - m).astype(BF16)) for t in tiles], axis=1)
        return _dot(p, v_all), jnp.exp2(sink - m)

    n_sub = ATTN_Q_BLOCKS if band else 1
    if band:
        pick = lambda pm, mid, nx, j, cols: (blk(pm, 0, cols) if j < 0 else
                                             blk(nx, 0, cols) if j >= n_sub else blk(mid, j, cols))
    for first in range(0, n_sub, ATTN_GROUP if band else 1):
        subs = range(first, min(first + ATTN_GROUP, n_sub)) if band else range(1)
        staged = {}
        for sub in subs:
            if band:
                ok_prev = jnp.logical_and(in_prev, n > 0) if sub == 0 else in_prev
                ok_next = jnp.logical_and(in_next, n < steps - 1) if sub == n_sub - 1 else in_next
            else:
                ok_prev = ok_next = None
            for h in range(N_KV_HEADS):
                ks = slice(h * LANES, (h + 1) * LANES)
                k_parts = [kc_ref[:, ks]]
                if band:
                    k_parts = [pick(kp_ref, km_ref, kn_ref, sub + d, ks) for d in (-1, 0, 1)] + k_parts
                staged[sub, h] = scores(sub * BLOCK, h, k_parts, ok_prev, ok_next)
        for sub in subs:
            if band:
                v_all = jnp.concatenate([pick(vp_ref, vm_ref, vn_ref, sub + d, every) for d in (-1, 0, 1)]
                                        + [vc_ref[...]], axis=0)
            else:
                v_all = vc_ref[...]
            v_low = lax.broadcasted_iota(jnp.int32, v_all.shape, 1) < HEAD_DIM
            ones = jnp.ones_like(v_all)
            v_aug = [jnp.where(v_low, v_all, ones), jnp.where(v_low, ones, v_all)]
            (o0, e0), (o1, e1) = [weights(*staged[sub, h], v_aug[h]) for h in range(N_KV_HEADS)]
            for g in range(Q_PER_KV):
                r = slice(g * BLOCK, (g + 1) * BLOCK)
                num = jnp.where(low, o0[r], o1[r])
                den = pltpu.roll(jnp.where(low, o1[r], o0[r]), HEAD_DIM, 1) + jnp.where(low, e0[r], e1[r])
                o_ref[sub * BLOCK:(sub + 1) * BLOCK, g * LANES:(g + 1) * LANES] = (num / den).astype(BF16)


def _attention(l, q, k, v, kc, vc, sink, band):
    B, T, _ = q.shape
    L = kc.shape[1]
    nq = ATTN_Q_BLOCKS if band else 1
    rows = nq * BLOCK
    steps = T // rows
    nb = T // BLOCK
    qspec = pl.BlockSpec((None, rows, ATTN_WIDTH), lambda b, n: (b, n, 0))
    kprev = lambda w: pl.BlockSpec((None, BLOCK, w), lambda b, n: (b, jnp.maximum(n * nq - 1, 0), 0))
    kmid = lambda w: pl.BlockSpec((None, rows, w), lambda b, n: (b, n, 0))
    knext = lambda w: pl.BlockSpec((None, BLOCK, w), lambda b, n: (b, jnp.minimum((n + 1) * nq, nb - 1), 0))
    cspec = lambda w: pl.BlockSpec((None, L, w), lambda b, n: (b, 0, 0))
    in_specs = [pl.BlockSpec(memory_space=pltpu.SMEM), qspec]
    args = [sink, q]
    if band:
        in_specs += [kprev(_K_COLS), kmid(_K_COLS), knext(_K_COLS),
                     kprev(KV_WIDTH), kmid(KV_WIDTH), knext(KV_WIDTH)]
        args += [k, k, k, v, v, v]
    in_specs += [cspec(_K_COLS), cspec(KV_WIDTH)]
    args += [kc, vc]
    return pl.pallas_call(
        functools.partial(_attn_kernel, layer=l, steps=steps, band=band),
        grid=(B, steps),
        in_specs=in_specs,
        out_specs=pl.BlockSpec((None, rows, ATTN_WIDTH), lambda b, n: (b, n, 0)),
        out_shape=jax.ShapeDtypeStruct((B, T, ATTN_WIDTH), BF16),
        compiler_params=_params("arbitrary", "arbitrary"),
        name="attention_band" if band else "attention_ctx",
    )(*args)


def _pool_kernel(u_ref, w_ref, s_ref, o_ref, pad_ref, *, T, tc):
    zeros = jnp.zeros((POOL_HALO, POOL_WIDTH), F32)
    pad_ref[0:POOL_HALO, :] = zeros
    pad_ref[POOL_HALO + T:POOL_HALO + T + POOL_HALO, :] = zeros
    pad_ref[POOL_HALO:POOL_HALO + T, :] = u_ref[...]
    lane = lax.broadcasted_iota(jnp.int32, (tc, LANES), 1)
    upper = lane >= POOL_GROUP_DIM
    row = lax.broadcasted_iota(jnp.int32, (tc, LANES), 0)
    for c in range(T // tc):
        base = POOL_HALO + c * tc
        t = row + c * tc
        pooled = []
        for tile in range(POOL_WIDTH // LANES):
            w_small = 2 << (2 * tile)
            w_big = 2 * w_small
            cols = slice(tile * LANES, (tile + 1) * LANES)
            load = lambda d: pad_ref[base + d:base + d + tc, cols]
            inner = load(-(w_small // 2))
            for d in range(-(w_small // 2) + 1, w_small // 2):
                inner = inner + load(d)
            outer = load(-(w_big // 2))
            for d in list(range(-(w_big // 2) + 1, -(w_small // 2))) + list(range(w_small // 2, w_big // 2)):
                outer = outer + load(d)
            win = inner + jnp.where(upper, outer, 0.0)
            half = jnp.where(upper, w_big // 2, w_small // 2)
            lo = jnp.maximum(t - half, 0)
            hi = jnp.minimum(t + half, T)
            cnt = (hi - lo).astype(F32)
            pooled.append(win / cnt - load(0))
        p = jnp.concatenate(pooled, axis=1).astype(BF16)
        y = _dot(p, w_ref[...]) * s_ref[...]
        o_ref[c * tc:(c + 1) * tc, :] = y.astype(BF16)


def _pool(l, u, w_bd, scale):
    B, T, _ = u.shape
    tc = min(TOKEN_TILE, T)
    return pl.pallas_call(
        functools.partial(_pool_kernel, T=T, tc=tc),
        grid=(B,),
        in_specs=[
            pl.BlockSpec((None, T, POOL_WIDTH), lambda b: (b, 0, 0)),
            _layer(w_bd, l),
            _layer(scale, l),
        ],
        out_specs=pl.BlockSpec((None, T, POOL_WIDTH), lambda b: (b, 0, 0)),
        out_shape=jax.ShapeDtypeStruct((B, T, POOL_WIDTH), BF16),
        scratch_shapes=[pltpu.VMEM((T + 2 * POOL_HALO, POOL_WIDTH), F32)],
        compiler_params=_params("arbitrary"),
        name="pool",
    )(u, w_bd, scale)


def _fourier_factors(T):
    n1 = 1 << (int(math.log2(T)) // 2)
    return n1, T // n1


def _split_const(a):
    a = jnp.asarray(a, F32)
    hi = a.astype(BF16)
    return hi, (a - hi.astype(F32)).astype(BF16)


def _fourier_tables(T):
    n1, n2 = _fourier_factors(T)
    gd = FOURIER_GROUP_DIM
    t2 = np.arange(n2)[:, None, None]
    k1 = np.arange(n1)[None, :, None]
    t1 = np.arange(n1)[None, None, :]
    ph = -2.0 * np.pi * (t2 * k1 / T + t1 * k1 / n1)
    m1 = np.concatenate([np.cos(ph), np.sin(ph)], axis=1) / math.sqrt(n1)
    a3 = 2.0 * np.pi * np.outer(np.arange(n2), np.arange(n2)) / n2
    c3, s3 = np.cos(a3) / math.sqrt(n2), np.sin(a3) / math.sqrt(n2)
    m3 = np.block([[c3, s3], [-s3, c3]])
    c = np.arange(gd)
    ang = 2.0 * np.pi * np.outer(c, c) / gd
    eye = np.eye(FOURIER_WIDTH // gd)
    cd = np.concatenate([np.kron(eye, np.cos(ang)), np.kron(eye, np.sin(ang))], axis=0) / math.sqrt(gd)
    return _split_const(m1) + _split_const(m3) + _split_const(cd)


def _split(a):
    hi = a.astype(BF16)
    return hi, (a - hi.astype(F32)).astype(BF16)


def _dot3(a, b):
    return _dot(a[0], b[0]) + (_dot(a[1], b[0]) + _dot(a[0], b[1]))


def _fourier_kernel(f_ref, m1h_ref, m1l_ref, m3h_ref, m3l_ref, cdh_ref, cdl_ref, o_ref, a_ref, g_ref,
                    *, n1, n2, tc, unroll):
    T = n1 * n2
    wt = FOURIER_WIDTH // LANES
    lanes = lambda j: slice(j * LANES, (j + 1) * LANES)
    pitch = n1 + FOURIER_ROW_PAD

    def over_t1(t2, carry):
        rows = pl.ds(t2, n1, stride=n2)
        x = jnp.concatenate([f_ref[j, rows, :] for j in range(wt)], axis=1)
        a = _dot3((m1h_ref[t2], m1l_ref[t2]), _split(x))
        off = pl.multiple_of(t2 * pitch, SUBLANES)
        for j in range(wt):
            a_ref[j, pl.ds(off, n1), :] = a[:n1, lanes(j)]
            a_ref[wt + j, pl.ds(off, n1), :] = a[n1:, lanes(j)]
        return carry

    lax.fori_loop(0, n2, over_t1, 0, unroll=unroll)

    def over_t2(k1, carry):
        rows = pl.ds(k1, n2, stride=pitch)
        b = jnp.concatenate(
            [jnp.concatenate([a_ref[h * wt + j, rows, :] for j in range(wt)], axis=1) for h in range(2)], axis=0)
        g = _dot3((m3h_ref[...], m3l_ref[...]), _split(b))
        for j in range(wt):
            g_ref[j, rows, :] = g[:n2, lanes(j)]
            g_ref[wt + j, rows, :] = g[n2:, lanes(j)]
        return carry

    lax.fori_loop(0, n1, over_t2, 0, unroll=unroll)

    per = tc // n1
    for c in range(T // tc):
        groups = [slice((c * per + i) * pitch, (c * per + i) * pitch + n1) for i in range(per)]
        g = jnp.concatenate([jnp.concatenate([g_ref[j, r, :] for r in groups], axis=0) for j in range(2 * wt)],
                            axis=1)
        o_ref[c * tc:(c + 1) * tc, :] = _dot3(_split(g), (cdh_ref[...], cdl_ref[...])).astype(BF16)


def _fourier(f, tables):
    B, wt, T, _ = f.shape
    n1, n2 = _fourier_factors(T)
    return pl.pallas_call(
        functools.partial(_fourier_kernel, n1=n1, n2=n2, tc=min(TOKEN_TILE, T), unroll=FOURIER_UNROLL),
        grid=(B,),
        in_specs=[pl.BlockSpec((None, wt, T, LANES), lambda b: (b, 0, 0, 0))] + [_resident(t.shape) for t in tables],
        out_specs=pl.BlockSpec((None, T, FOURIER_WIDTH), lambda b: (b, 0, 0)),
        out_shape=jax.ShapeDtypeStruct((B, T, FOURIER_WIDTH), BF16),
        scratch_shapes=[pltpu.VMEM((2 * wt, n2 * (n1 + FOURIER_ROW_PAD), LANES), F32)] * 2,
        compiler_params=_params("arbitrary"),
        name="fourier",
    )(f, *tables)


def _merge_kernel(x_ref, o_ref, p_ref, f_ref, g_ref, sh_ref, sc_ref, gt_ref,
                  wg_ref, wa_ref, wp_ref, wf_ref, wo_ref, out_ref, y_ref):
    x = x_ref[...]
    h = _modnorm(x, g_ref[...], sc_ref[...], sh_ref[...]).astype(BF16)
    o = o_ref[...]
    p = p_ref[...]
    f = f_ref[...]
    for j in range(D_MODEL // MERGE_CHUNK):
        c = slice(j * MERGE_CHUNK, (j + 1) * MERGE_CHUNK)
        gcol = lambda b: IN_TOKEN_COLS + b * D_MODEL + j * MERGE_CHUNK
        gate = lambda b: jax.nn.sigmoid(_dot(h, wg_ref[:, gcol(b):gcol(b) + MERGE_CHUNK]))
        y = (gate(0) * _dot(o, wa_ref[:, c]) + gate(1) * _dot(p, wp_ref[:, c])
             + gate(2) * _dot(f, wf_ref[:, c]))
        y_ref[:, c] = y.astype(BF16)
    out_ref[...] = x + gt_ref[...] * _dot(y_ref[...], wo_ref[...])


def _merge(l, x, o, p, f, mod, gain, wg, wa, wp, wf, wo, ctx_row):
    B, T, _ = x.shape
    tm = min(MERGE_TILE, T)
    tok = lambda w: pl.BlockSpec((None, tm, w), lambda b, i: (b, i, 0))
    return pl.pallas_call(
        _merge_kernel,
        grid=(B, T // tm),
        in_specs=[
            tok(D_MODEL), tok(ATTN_WIDTH), tok(POOL_WIDTH), tok(FOURIER_WIDTH),
            _layer(gain, l),
            _mod_spec(l, 0, ctx_row), _mod_spec(l, 1, ctx_row), _mod_spec(l, 2, ctx_row),
            _layer(wg, l), _layer(wa, l), _layer(wp, l), _layer(wf, l), _layer(wo, l),
        ],
        out_specs=tok(D_MODEL),
        out_shape=jax.ShapeDtypeStruct((B, T, D_MODEL), F32),
        scratch_shapes=[pltpu.VMEM((tm, D_MODEL), BF16)],
        compiler_params=_params("arbitrary", "arbitrary"),
        name="merge",
    )(x, o, p, f, gain, mod, mod, mod, wg, wa, wp, wf, wo)


def _ffn_kernel(x_ref, xp_ref, xn_ref, g_ref, sh_ref, sc_ref, gt_ref, wu_ref, cw_ref, wd_ref, gf_ref,
                out_ref, h_ref, a_ref, *, tm, seg, final):
    i = pl.program_id(1)
    last = pl.num_programs(1) - 1
    g, sh, sc = g_ref[...], sh_ref[...], sc_ref[...]
    x = x_ref[...]
    H = SUBLANES
    h_ref[H:H + tm, :] = _modnorm(x, g, sc, sh).astype(BF16)
    keep_prev = (i > 0).astype(F32)
    keep_next = (i < last).astype(F32)
    h_ref[0:H, :] = (_modnorm(xp_ref[...], g, sc, sh) * keep_prev).astype(BF16)
    h_ref[H + tm:H + tm + H, :] = (_modnorm(xn_ref[...], g, sc, sh) * keep_next).astype(BF16)
    hf = h_ref[...]
    rows = tm + 2 * H

    if seg is not None:
        pos = (lax.broadcasted_iota(jnp.int32, (rows, 1), 0) - H) % seg
        tap_prev = (pos != 0).astype(F32)
        tap_next = (pos != seg - 1).astype(F32)

    def conv(up, cols):
        w = cw_ref[:, cols]
        before, after = pltpu.roll(up, 1, 0), pltpu.roll(up, rows - 1, 0)
        if seg is not None:
            before, after = before * tap_prev, after * tap_next
        y = before * w[0:1] + up * w[1:2] + after * w[2:3]
        return y[H:H + tm]

    for j in range(D_FF // FF_CHUNK):
        cv = slice(j * FF_CHUNK, (j + 1) * FF_CHUNK)
        cg = slice(D_FF + j * FF_CHUNK, D_FF + (j + 1) * FF_CHUNK)
        val = conv(_dot(hf, wu_ref[:, cv]), cv)
        gate = conv(_dot(hf, wu_ref[:, cg]), cg)
        a_ref[:, cv] = (val * (gate * jax.nn.sigmoid(gate))).astype(BF16)
    y = x + gt_ref[...] * _dot(a_ref[...], wd_ref[...])
    if final:
        ms = jnp.mean(y * y, axis=-1, keepdims=True)
        y = y * lax.rsqrt(ms + EPS) * gf_ref[...]
    out_ref[...] = y


def _ffn(l, x, mod, gain, wu, cw, wd, gain_final, ctx_row, final, seg=None):
    B, T, _ = x.shape
    tm = min(FFN_TILE, T)
    assert seg is None or (tm == T and T % seg == 0)
    per = tm // SUBLANES
    nblk = T // SUBLANES
    tok = pl.BlockSpec((None, tm, D_MODEL), lambda b, i: (b, i, 0))
    prev = pl.BlockSpec((None, SUBLANES, D_MODEL), lambda b, i: (b, jnp.maximum(i * per - 1, 0), 0))
    nxt = pl.BlockSpec((None, SUBLANES, D_MODEL), lambda b, i: (b, jnp.minimum((i + 1) * per, nblk - 1), 0))
    return pl.pallas_call(
        functools.partial(_ffn_kernel, tm=tm, seg=seg, final=final),
        grid=(B, T // tm),
        in_specs=[
            tok, prev, nxt,
            _layer(gain, l),
            _mod_spec(l, 3, ctx_row), _mod_spec(l, 4, ctx_row), _mod_spec(l, 5, ctx_row),
            _layer(wu, l), _layer(cw, l), _layer(wd, l),
            _resident((1, D_MODEL)),
        ],
        out_specs=tok,
        out_shape=jax.ShapeDtypeStruct((B, T, D_MODEL), F32),
        scratch_shapes=[pltpu.VMEM((tm + 2 * SUBLANES, D_MODEL), BF16), pltpu.VMEM((tm, D_FF), BF16)],
        compiler_params=_params("arbitrary", "arbitrary"),
        name="ffn_final" if final else "ffn",
    )(x, x, x, gain, mod, mod, mod, wu, cw, wd, gain_final)


def _rope_tables(T):
    rows = T // GRID_W
    row = np.repeat(np.arange(rows), GRID_W).astype(np.float64)
    col = np.tile(np.arange(GRID_W), rows).astype(np.float64)
    n_freq = HEAD_DIM // 4
    inv_freq = ROPE_BASE ** (-np.arange(n_freq) / n_freq)
    ar, ac = row[:, None] * inv_freq[None, :], col[:, None] * inv_freq[None, :]
    cos = np.concatenate([np.cos(ar), np.cos(ar), np.cos(ac), np.cos(ac)], axis=1)
    sin = np.concatenate([-np.sin(ar), np.sin(ar), -np.sin(ac), np.sin(ac)], axis=1)
    reps = LANES // HEAD_DIM
    return jnp.asarray(np.tile(cos, (1, reps)), F32), jnp.asarray(np.tile(sin, (1, reps)), F32)


def kernel(x, c, ctx, c_ctx, w_mod, b_mod, norm_mix, norm_ffn, w_in, attn_sink, pool_w, pool_scale,
           w_br_attn, w_br_pool, w_br_four, w_out, w_up, conv_w, w_down, norm_final):
    B, T, _ = x.shape
    L = ctx.shape[1]
    depth = w_mod.shape[0]
    assert B < COND_ROWS and T % max(INPROJ_TILE, MERGE_TILE, FFN_TILE) == 0 and T % (ATTN_Q_BLOCKS * BLOCK) == 0
    assert L % BLOCK == 0

    cond = jnp.concatenate([c, c_ctx[None], jnp.zeros((COND_ROWS - B - 1, D_MODEL), F32)], axis=0)
    mod = _adaln(cond, w_mod, b_mod).reshape(depth, COND_ROWS, N_MOD, 1, D_MODEL)

    cos_t, sin_t = _rope_tables(T)
    cos_c = jnp.ones((B * L, LANES), F32)
    sin_c = jnp.zeros((B * L, LANES), F32)
    four_lat = _fourier_tables(T)
    four_ctx = _fourier_tables(L)

    w_a = w_in.astype(BF16)
    w_a_br = (w_br_attn.reshape(depth, N_KV_HEADS, Q_PER_KV, HEAD_DIM, D_MODEL)
              .transpose(0, 2, 1, 3, 4).reshape(depth, ATTN_WIDTH, D_MODEL).astype(BF16))
    w_p_br = w_br_pool.astype(BF16)
    w_f_br = w_br_four.astype(BF16)
    w_o = w_out.astype(BF16)
    groups = pool_w.shape[1]
    w_pool = jnp.einsum("lgab,gh->lgahb", pool_w, jnp.eye(groups, dtype=pool_w.dtype)).reshape(
        depth, POOL_WIDTH, POOL_WIDTH).astype(BF16)
    p_scale = pool_scale.reshape(depth, 1, POOL_WIDTH)
    w_u = w_up.astype(BF16)
    w_d = w_down.astype(BF16)
    g_mix = norm_mix.reshape(depth, 1, D_MODEL)
    g_ffn = norm_ffn.reshape(depth, 1, D_MODEL)
    gain_final = norm_final.reshape(1, D_MODEL)

    xc = ctx
    for l in range(depth):
        last = l == depth - 1
        k, v, q, u, f = _inproj(l, x, mod, g_mix, w_a, cos_t, sin_t, None)
        flat = lambda a: a.reshape((1, B * L) + a.shape[2:])
        kc, vc, qc, uc, fc = _inproj(l, flat(xc), mod, g_mix, w_a, cos_c, sin_c, B)
        kc, vc, qc, uc = (a.reshape((B, L) + a.shape[2:]) for a in (kc, vc, qc, uc))
        fc = fc.reshape(FOURIER_WIDTH // LANES, B, L, LANES).transpose(1, 0, 2, 3)
        o = _attention(l, q, k, v, kc, vc, attn_sink, True)
        x = _merge(l, x, o, _pool(l, u, w_pool, p_scale), _fourier(f, four_lat), mod, g_mix,
                   w_a, w_a_br, w_p_br, w_f_br, w_o, None)
        if not last:
            oc = _attention(l, qc, None, None, kc, vc, attn_sink, False)
            xc = _merge(l, flat(xc), flat(oc), flat(_pool(l, uc, w_pool, p_scale)), flat(_fourier(fc, four_ctx)),
                        mod, g_mix, w_a, w_a_br, w_p_br, w_f_br, w_o, B).reshape(B, L, D_MODEL)
        x = _ffn(l, x, mod, g_ffn, w_u, conv_w, w_d, gain_final, None, last)
        if not last:
            xc = _ffn(l, flat(xc), mod, g_ffn, w_u, conv_w, w_d, gain_final, B, False, seg=L).reshape(B, L, D_MODEL)
    return x
```

```python
import functools
import math

import numpy as np
import jax
import jax.numpy as jnp
from jax import lax
from jax.experimental import pallas as pl
from jax.experimental.pallas import tpu as pltpu

D_MODEL = 1024
N_Q_HEADS = 8
N_KV_HEADS = 2
HEAD_DIM = 64
Q_PER_KV = N_Q_HEADS // N_KV_HEADS
ATTN_WIDTH = N_Q_HEADS * HEAD_DIM
KV_WIDTH = N_KV_HEADS * HEAD_DIM
BLOCK = 128
GRID_W = 64
ROPE_BASE = 10000.0
POOL_WIDTH = 256
POOL_GROUP_DIM = 64
FOURIER_WIDTH = 256
FOURIER_GROUP_DIM = 64
D_FF = 2816
N_MOD = 6
EPS = 1e-6
NEG_INF = -1e30
LOG2E = math.log2(math.e)

LANES = 128
SUBLANES = 8
VMEM_LIMIT_BYTES = 56 * 1024 * 1024

INPROJ_TILE = 1024
TOKEN_TILE = 512
MERGE_TILE = 1024
FFN_TILE = 1024
ATTN_Q_BLOCKS = 8
ATTN_GROUP = 1
FF_CHUNK = 256
MERGE_CHUNK = 256
FOURIER_ROW_PAD = 8
FOURIER_UNROLL = 8
POOL_HALO = 8
ADALN_TILE = 1536
COND_ROWS = 8

F32 = jnp.float32
BF16 = jnp.bfloat16


def _dot(a, b):
    return jnp.dot(a, b, preferred_element_type=F32)


def _dot_nt(a, b):
    return lax.dot_general(a, b, (((1,), (1,)), ((), ())), preferred_element_type=F32)


def _modnorm(x, g, sc, sh):
    ms = jnp.mean(x * x, axis=-1, keepdims=True)
    y = x * lax.rsqrt(ms + EPS) * g
    return y * (1.0 + sc) + sh


def _resident(shape):
    nd = len(shape)
    return pl.BlockSpec(shape, lambda *_: (0,) * nd, pipeline_mode=pl.Buffered(1))


def _layer(arr, l):
    nd = arr.ndim - 1
    return pl.BlockSpec((None,) + arr.shape[1:], lambda *_: (l,) + (0,) * nd, pipeline_mode=pl.Buffered(1))


def _mod_spec(l, j, ctx_row):
    if ctx_row is None:
        return pl.BlockSpec((None, None, None, 1, D_MODEL), lambda b, i: (l, b, j, 0, 0))
    return pl.BlockSpec((None, None, None, 1, D_MODEL), lambda b, i: (l, ctx_row, j, 0, 0))


def _params(*sem):
    return pltpu.CompilerParams(dimension_semantics=sem, vmem_limit_bytes=VMEM_LIMIT_BYTES)


def _adaln_kernel(c_ref, w_ref, b_ref, o_ref):
    c = c_ref[...]
    a = (c * jax.nn.sigmoid(c)).astype(BF16)
    o_ref[...] = _dot(a, w_ref[...].astype(BF16)) + b_ref[...]


def _adaln(cond, w_mod, b_mod):
    depth, _, width = w_mod.shape
    return pl.pallas_call(
        _adaln_kernel,
        grid=(depth, width // ADALN_TILE),
        in_specs=[
            pl.BlockSpec((COND_ROWS, D_MODEL), lambda l, j: (0, 0)),
            pl.BlockSpec((None, D_MODEL, ADALN_TILE), lambda l, j: (l, 0, j)),
            pl.BlockSpec((None, 1, ADALN_TILE), lambda l, j: (l, 0, j)),
        ],
        out_specs=pl.BlockSpec((None, COND_ROWS, ADALN_TILE), lambda l, j: (l, 0, j)),
        out_shape=jax.ShapeDtypeStruct((depth, COND_ROWS, width), F32),
        compiler_params=_params("arbitrary", "arbitrary"),
        name="adaln",
    )(cond, w_mod, b_mod.reshape(depth, 1, width))


_K_COLS = 2 * KV_WIDTH
IN_TOKEN_COLS = 2 * KV_WIDTH + ATTN_WIDTH + POOL_WIDTH + FOURIER_WIDTH
_A_SPLITS = (0, KV_WIDTH, 2 * KV_WIDTH, 2 * KV_WIDTH + ATTN_WIDTH, 2 * KV_WIDTH + ATTN_WIDTH + POOL_WIDTH,
             IN_TOKEN_COLS)


def _inproj_kernel(x_ref, g_ref, sh_ref, sc_ref, w_ref, cos_ref, sin_ref,
                   k_ref, v_ref, q_ref, u_ref, f_ref):
    h = _modnorm(x_ref[...], g_ref[...], sc_ref[...], sh_ref[...]).astype(BF16)
    cos = cos_ref[...]
    sin = sin_ref[...]

    lane = lax.broadcasted_iota(jnp.int32, cos.shape, 1)
    first = (lane % (HEAD_DIM // 2)) < HEAD_DIM // 4
    low = lane < HEAD_DIM

    def rope(t):
        partner = jnp.where(first, pltpu.roll(t, LANES - HEAD_DIM // 4, 1), pltpu.roll(t, HEAD_DIM // 4, 1))
        return t * cos + partner * sin

    k0, v0, q0, u0, f0, end = _A_SPLITS
    tile = lambda a, j: a[:, j * LANES:(j + 1) * LANES]
    kvq = _dot(h, w_ref[:, k0:u0])
    k = rope(tile(kvq, 0))
    k_swapped = pltpu.roll(k, HEAD_DIM, 1)
    k_ref[:, 0:LANES] = jnp.where(low, k, k_swapped).astype(BF16)
    k_ref[:, LANES:2 * LANES] = jnp.where(low, k_swapped, k).astype(BF16)
    v_ref[...] = tile(kvq, v0 // LANES).astype(BF16)
    scale = HEAD_DIM ** -0.5 * LOG2E
    for j in range(ATTN_WIDTH // LANES):
        q_ref[:, j * LANES:(j + 1) * LANES] = (rope(tile(kvq, q0 // LANES + j)) * scale).astype(BF16)
    uf = _dot(h, w_ref[:, u0:end])
    u_ref[...] = uf[:, 0:f0 - u0]
    for j in range(FOURIER_WIDTH // LANES):
        f_ref[j] = tile(uf, (f0 - u0) // LANES + j)


def _inproj(l, x, mod, gain, w_a, cos_t, sin_t, ctx_row):
    B, T, _ = x.shape
    tm = min(INPROJ_TILE, T)
    tok = lambda w: pl.BlockSpec((None, tm, w), lambda b, i: (b, i, 0))
    return pl.pallas_call(
        _inproj_kernel,
        grid=(B, T // tm),
        in_specs=[
            tok(D_MODEL),
            _layer(gain, l),
            _mod_spec(l, 0, ctx_row), _mod_spec(l, 1, ctx_row),
            pl.BlockSpec((None, D_MODEL, IN_TOKEN_COLS), lambda b, i: (l, 0, 0), pipeline_mode=pl.Buffered(1)),
            pl.BlockSpec((tm, LANES), lambda b, i: (i, 0)),
            pl.BlockSpec((tm, LANES), lambda b, i: (i, 0)),
        ],
        out_specs=[tok(_K_COLS), tok(KV_WIDTH), tok(ATTN_WIDTH), tok(POOL_WIDTH),
                   pl.BlockSpec((None, FOURIER_WIDTH // LANES, tm, LANES), lambda b, i: (b, 0, i, 0))],
        out_shape=[
            jax.ShapeDtypeStruct((B, T, _K_COLS), BF16),
            jax.ShapeDtypeStruct((B, T, KV_WIDTH), BF16),
            jax.ShapeDtypeStruct((B, T, ATTN_WIDTH), BF16),
            jax.ShapeDtypeStruct((B, T, POOL_WIDTH), F32),
            jax.ShapeDtypeStruct((B, FOURIER_WIDTH // LANES, T, LANES), F32),
        ],
        compiler_params=_params("arbitrary", "arbitrary"),
        name="inproj",
    )(x, gain, mod, mod, w_a, cos_t, sin_t)


def _attn_kernel(sink_ref, q_ref, *refs, layer, steps, band):
    if band:
        kp_ref, km_ref, kn_ref, vp_ref, vm_ref, vn_ref, kc_ref, vc_ref, o_ref = refs
    else:
        kc_ref, vc_ref, o_ref = refs
    n = pl.program_id(1)
    rows = Q_PER_KV * BLOCK
    lane = lax.broadcasted_iota(jnp.int32, (BLOCK, LANES), 1)
    low = lane < HEAD_DIM
    rgrp = lax.broadcasted_iota(jnp.int32, (rows, 1), 0) // BLOCK
    if band:
        qi = lax.broadcasted_iota(jnp.int32, (rows, BLOCK), 0) % BLOCK
        kj = lax.broadcasted_iota(jnp.int32, (rows, BLOCK), 1)
        in_prev = kj >= qi
        in_next = kj <= qi
    blk = lambda ref, j, cols: ref[j * BLOCK:(j + 1) * BLOCK, cols]
    every = slice(None)

    def scores(q0, h, k_parts, ok_prev, ok_next):
        qs = []
        for t in range(Q_PER_KV // 2):
            c0 = (h * (Q_PER_KV // 2) + t) * LANES
            qt = q_ref[q0:q0 + BLOCK, c0:c0 + LANES]
            qs.append(jnp.where(low, qt, jnp.zeros_like(qt)))
            qs.append(jnp.where(low, jnp.zeros_like(qt), qt))
        qh = jnp.concatenate(qs, axis=0)
        s = _dot_nt(qh, jnp.concatenate(k_parts, axis=0))
        tiles = [s[:, j * LANES:(j + 1) * LANES] for j in range(s.shape[1] // LANES)]
        if band:
            tiles[0] = jnp.where(ok_prev, tiles[0], NEG_INF)
            tiles[2] = jnp.where(ok_next, tiles[2], NEG_INF)
        sink = jnp.zeros((rows, 1), F32)
        for g in range(Q_PER_KV):
            sink = jnp.where(rgrp == g, sink_ref[layer, h * Q_PER_KV + g] * LOG2E, sink)
        m_t = tiles[0]
        for t in tiles[1:]:
            m_t = jnp.maximum(m_t, t)
        m = jnp.maximum(jnp.max(m_t, axis=-1, keepdims=True), sink)
        return tiles, m, sink

    def weights(tiles, m, sink, v_all):
        p = jnp.concatenate([jnp.exp2((t - m).astype(BF16)) for t in tiles], axis=1)
        return _dot(p, v_all), jnp.exp2(sink - m)

    n_sub = ATTN_Q_BLOCKS if band else 1
    if band:
        pick = lambda pm, mid, nx, j, cols: (blk(pm, 0, cols) if j < 0 else
                                             blk(nx, 0, cols) if j >= n_sub else blk(mid, j, cols))

    def score_group(subs):
        staged = {}
        for sub in subs:
            if band:
                ok_prev = jnp.logical_and(in_prev, n > 0) if sub == 0 else in_prev
                ok_next = jnp.logical_and(in_next, n < steps - 1) if sub == n_sub - 1 else in_next
            else:
                ok_prev = ok_next = None
            for h in range(N_KV_HEADS):
                ks = slice(h * LANES, (h + 1) * LANES)
                k_parts = [kc_ref[:, ks]]
                if band:
                    k_parts = [pick(kp_ref, km_ref, kn_ref, sub + d, ks) for d in (-1, 0, 1)] + k_parts
                staged[sub, h] = scores(sub * BLOCK, h, k_parts, ok_prev, ok_next)
        return staged

    def value_group(subs, staged):
        for sub in subs:
            if band:
                v_all = jnp.concatenate([pick(vp_ref, vm_ref, vn_ref, sub + d, every) for d in (-1, 0, 1)]
                                        + [vc_ref[...]], axis=0)
            else:
                v_all = vc_ref[...]
            v_low = lax.broadcasted_iota(jnp.int32, v_all.shape, 1) < HEAD_DIM
            ones = jnp.ones_like(v_all)
            v_aug = [jnp.where(v_low, v_all, ones), jnp.where(v_low, ones, v_all)]
            (o0, e0), (o1, e1) = [weights(*staged[sub, h], v_aug[h]) for h in range(N_KV_HEADS)]
            for g in range(Q_PER_KV):
                r = slice(g * BLOCK, (g + 1) * BLOCK)
                num = jnp.where(low, o0[r], o1[r])
                den = pltpu.roll(jnp.where(low, o1[r], o0[r]), HEAD_DIM, 1) + jnp.where(low, e0[r], e1[r])
                o_ref[sub * BLOCK:(sub + 1) * BLOCK, g * LANES:(g + 1) * LANES] = (num / den).astype(BF16)

    size = ATTN_GROUP if band else 1
    pending = None
    for first in range(0, n_sub, size):
        subs = range(first, min(first + size, n_sub))
        staged = score_group(subs)
        if pending is not None:
            value_group(*pending)
        pending = (subs, staged)
    value_group(*pending)


def _attention(l, q, k, v, kc, vc, sink, band):
    B, T, _ = q.shape
    L = kc.shape[1]
    nq = ATTN_Q_BLOCKS if band else 1
    rows = nq * BLOCK
    steps = T // rows
    nb = T // BLOCK
    qspec = pl.BlockSpec((None, rows, ATTN_WIDTH), lambda b, n: (b, n, 0))
    kprev = lambda w: pl.BlockSpec((None, BLOCK, w), lambda b, n: (b, jnp.maximum(n * nq - 1, 0), 0))
    kmid = lambda w: pl.BlockSpec((None, rows, w), lambda b, n: (b, n, 0))
    knext = lambda w: pl.BlockSpec((None, BLOCK, w), lambda b, n: (b, jnp.minimum((n + 1) * nq, nb - 1), 0))
    cspec = lambda w: pl.BlockSpec((None, L, w), lambda b, n: (b, 0, 0))
    in_specs = [pl.BlockSpec(memory_space=pltpu.SMEM), qspec]
    args = [sink, q]
    if band:
        in_specs += [kprev(_K_COLS), kmid(_K_COLS), knext(_K_COLS),
                     kprev(KV_WIDTH), kmid(KV_WIDTH), knext(KV_WIDTH)]
        args += [k, k, k, v, v, v]
    in_specs += [cspec(_K_COLS), cspec(KV_WIDTH)]
    args += [kc, vc]
    return pl.pallas_call(
        functools.partial(_attn_kernel, layer=l, steps=steps, band=band),
        grid=(B, steps),
        in_specs=in_specs,
        out_specs=pl.BlockSpec((None, rows, ATTN_WIDTH), lambda b, n: (b, n, 0)),
        out_shape=jax.ShapeDtypeStruct((B, T, ATTN_WIDTH), BF16),
        compiler_params=_params("arbitrary", "arbitrary"),
        name="attention_band" if band else "attention_ctx",
    )(*args)


def _pool_kernel(u_ref, w_ref, s_ref, o_ref, pad_ref, *, T, tc):
    zeros = jnp.zeros((POOL_HALO, POOL_WIDTH), F32)
    pad_ref[0:POOL_HALO, :] = zeros
    pad_ref[POOL_HALO + T:POOL_HALO + T + POOL_HALO, :] = zeros
    pad_ref[POOL_HALO:POOL_HALO + T, :] = u_ref[...]
    lane = lax.broadcasted_iota(jnp.int32, (tc, LANES), 1)
    upper = lane >= POOL_GROUP_DIM
    row = lax.broadcasted_iota(jnp.int32, (tc, LANES), 0)
    for c in range(T // tc):
        base = POOL_HALO + c * tc
        t = row + c * tc
        pooled = []
        for tile in range(POOL_WIDTH // LANES):
            w_small = 2 << (2 * tile)
            w_big = 2 * w_small
            cols = slice(tile * LANES, (tile + 1) * LANES)
            load = lambda d: pad_ref[base + d:base + d + tc, cols]
            inner = load(-(w_small // 2))
            for d in range(-(w_small // 2) + 1, w_small // 2):
                inner = inner + load(d)
            outer = load(-(w_big // 2))
            for d in list(range(-(w_big // 2) + 1, -(w_small // 2))) + list(range(w_small // 2, w_big // 2)):
                outer = outer + load(d)
            win = inner + jnp.where(upper, outer, 0.0)
            half = jnp.where(upper, w_big // 2, w_small // 2)
            lo = jnp.maximum(t - half, 0)
            hi = jnp.minimum(t + half, T)
            cnt = (hi - lo).astype(F32)
            pooled.append(win / cnt - load(0))
        p = jnp.concatenate(pooled, axis=1).astype(BF16)
        y = _dot(p, w_ref[...]) * s_ref[...]
        o_ref[c * tc:(c + 1) * tc, :] = y.astype(BF16)


def _pool(l, u, w_bd, scale):
    B, T, _ = u.shape
    tc = min(TOKEN_TILE, T)
    return pl.pallas_call(
        functools.partial(_pool_kernel, T=T, tc=tc),
        grid=(B,),
        in_specs=[
            pl.BlockSpec((None, T, POOL_WIDTH), lambda b: (b, 0, 0)),
            _layer(w_bd, l),
            _layer(scale, l),
        ],
        out_specs=pl.BlockSpec((None, T, POOL_WIDTH), lambda b: (b, 0, 0)),
        out_shape=jax.ShapeDtypeStruct((B, T, POOL_WIDTH), BF16),
        scratch_shapes=[pltpu.VMEM((T + 2 * POOL_HALO, POOL_WIDTH), F32)],
        compiler_params=_params("arbitrary"),
        name="pool",
    )(u, w_bd, scale)


def _fourier_factors(T):
    n1 = 1 << (int(math.log2(T)) // 2)
    return n1, T // n1


def _split_const(a):
    a = jnp.asarray(a, F32)
    hi = a.astype(BF16)
    return hi, (a - hi.astype(F32)).astype(BF16)


def _fourier_tables(T):
    n1, n2 = _fourier_factors(T)
    gd = FOURIER_GROUP_DIM
    t2 = np.arange(n2)[:, None, None]
    k1 = np.arange(n1)[None, :, None]
    t1 = np.arange(n1)[None, None, :]
    ph = -2.0 * np.pi * (t2 * k1 / T + t1 * k1 / n1)
    m1 = np.concatenate([np.cos(ph), np.sin(ph)], axis=1) / math.sqrt(n1)
    a3 = 2.0 * np.pi * np.outer(np.arange(n2), np.arange(n2)) / n2
    c3, s3 = np.cos(a3) / math.sqrt(n2), np.sin(a3) / math.sqrt(n2)
    m3 = np.block([[c3, s3], [-s3, c3]])
    c = np.arange(gd)
    ang = 2.0 * np.pi * np.outer(c, c) / gd
    eye = np.eye(FOURIER_WIDTH // gd)
    cd = np.concatenate([np.kron(eye, np.cos(ang)), np.kron(eye, np.sin(ang))], axis=0) / math.sqrt(gd)
    return _split_const(m1) + _split_const(m3) + _split_const(cd)


def _split(a):
    hi = a.astype(BF16)
    return hi, (a - hi.astype(F32)).astype(BF16)


def _dot3(a, b):
    return _dot(a[0], b[0]) + (_dot(a[1], b[0]) + _dot(a[0], b[1]))


def _fourier_kernel(f_ref, m1h_ref, m1l_ref, m3h_ref, m3l_ref, cdh_ref, cdl_ref, o_ref, a_ref, g_ref,
                    *, n1, n2, tc, unroll):
    T = n1 * n2
    wt = FOURIER_WIDTH // LANES
    lanes = lambda j: slice(j * LANES, (j + 1) * LANES)
    pitch = n1 + FOURIER_ROW_PAD

    def over_t1(t2, carry):
        rows = pl.ds(t2, n1, stride=n2)
        x = jnp.concatenate([f_ref[j, rows, :] for j in range(wt)], axis=1)
        a = _dot3((m1h_ref[t2], m1l_ref[t2]), _split(x))
        off = pl.multiple_of(t2 * pitch, SUBLANES)
        for j in range(wt):
            a_ref[j, pl.ds(off, n1), :] = a[:n1, lanes(j)]
            a_ref[wt + j, pl.ds(off, n1), :] = a[n1:, lanes(j)]
        return carry

    lax.fori_loop(0, n2, over_t1, 0, unroll=unroll)

    def over_t2(k1, carry):
        rows = pl.ds(k1, n2, stride=pitch)
        b = jnp.concatenate(
            [jnp.concatenate([a_ref[h * wt + j, rows, :] for j in range(wt)], axis=1) for h in range(2)], axis=0)
        g = _dot3((m3h_ref[...], m3l_ref[...]), _split(b))
        for j in range(wt):
            g_ref[j, rows, :] = g[:n2, lanes(j)]
            g_ref[wt + j, rows, :] = g[n2:, lanes(j)]
        return carry

    lax.fori_loop(0, n1, over_t2, 0, unroll=unroll)

    per = tc // n1
    for c in range(T // tc):
        groups = [slice((c * per + i) * pitch, (c * per + i) * pitch + n1) for i in range(per)]
        g = jnp.concatenate([jnp.concatenate([g_ref[j, r, :] for r in groups], axis=0) for j in range(2 * wt)],
                            axis=1)
        o_ref[c * tc:(c + 1) * tc, :] = _dot3(_split(g), (cdh_ref[...], cdl_ref[...])).astype(BF16)


def _fourier(f, tables):
    B, wt, T, _ = f.shape
    n1, n2 = _fourier_factors(T)
    return pl.pallas_call(
        functools.partial(_fourier_kernel, n1=n1, n2=n2, tc=min(TOKEN_TILE, T), unroll=FOURIER_UNROLL),
        grid=(B,),
        in_specs=[pl.BlockSpec((None, wt, T, LANES), lambda b: (b, 0, 0, 0))] + [_resident(t.shape) for t in tables],
        out_specs=pl.BlockSpec((None, T, FOURIER_WIDTH), lambda b: (b, 0, 0)),
        out_shape=jax.ShapeDtypeStruct((B, T, FOURIER_WIDTH), BF16),
        scratch_shapes=[pltpu.VMEM((2 * wt, n2 * (n1 + FOURIER_ROW_PAD), LANES), F32)] * 2,
        compiler_params=_params("arbitrary"),
        name="fourier",
    )(f, *tables)


def _merge_kernel(x_ref, o_ref, p_ref, f_ref, g_ref, sh_ref, sc_ref, gt_ref,
                  wg_ref, wa_ref, wp_ref, wf_ref, wo_ref, out_ref, y_ref):
    x = x_ref[...]
    h = _modnorm(x, g_ref[...], sc_ref[...], sh_ref[...]).astype(BF16)
    o = o_ref[...]
    p = p_ref[...]
    f = f_ref[...]
    for j in range(D_MODEL // MERGE_CHUNK):
        c = slice(j * MERGE_CHUNK, (j + 1) * MERGE_CHUNK)
        gcol = lambda b: IN_TOKEN_COLS + b * D_MODEL + j * MERGE_CHUNK
        gate = lambda b: jax.nn.sigmoid(_dot(h, wg_ref[:, gcol(b):gcol(b) + MERGE_CHUNK]))
        y = (gate(0) * _dot(o, wa_ref[:, c]) + gate(1) * _dot(p, wp_ref[:, c])
             + gate(2) * _dot(f, wf_ref[:, c]))
        y_ref[:, c] = y.astype(BF16)
    out_ref[...] = x + gt_ref[...] * _dot(y_ref[...], wo_ref[...])


def _merge(l, x, o, p, f, mod, gain, wg, wa, wp, wf, wo, ctx_row):
    B, T, _ = x.shape
    tm = min(MERGE_TILE, T)
    tok = lambda w: pl.BlockSpec((None, tm, w), lambda b, i: (b, i, 0))
    return pl.pallas_call(
        _merge_kernel,
        grid=(B, T // tm),
        in_specs=[
            tok(D_MODEL), tok(ATTN_WIDTH), tok(POOL_WIDTH), tok(FOURIER_WIDTH),
            _layer(gain, l),
            _mod_spec(l, 0, ctx_row), _mod_spec(l, 1, ctx_row), _mod_spec(l, 2, ctx_row),
            _layer(wg, l), _layer(wa, l), _layer(wp, l), _layer(wf, l), _layer(wo, l),
        ],
        out_specs=tok(D_MODEL),
        out_shape=jax.ShapeDtypeStruct((B, T, D_MODEL), F32),
        scratch_shapes=[pltpu.VMEM((tm, D_MODEL), BF16)],
        compiler_params=_params("arbitrary", "arbitrary"),
        name="merge",
    )(x, o, p, f, gain, mod, mod, mod, wg, wa, wp, wf, wo)


def _ffn_kernel(x_ref, xp_ref, xn_ref, g_ref, sh_ref, sc_ref, gt_ref, wu_ref, cw_ref, wd_ref, gf_ref,
                out_ref, h_ref, a_ref, *, tm, seg, final):
    i = pl.program_id(1)
    last = pl.num_programs(1) - 1
    g, sh, sc = g_ref[...], sh_ref[...], sc_ref[...]
    x = x_ref[...]
    H = SUBLANES
    h_ref[H:H + tm, :] = _modnorm(x, g, sc, sh).astype(BF16)
    keep_prev = (i > 0).astype(F32)
    keep_next = (i < last).astype(F32)
    h_ref[0:H, :] = (_modnorm(xp_ref[...], g, sc, sh) * keep_prev).astype(BF16)
    h_ref[H + tm:H + tm + H, :] = (_modnorm(xn_ref[...], g, sc, sh) * keep_next).astype(BF16)
    hf = h_ref[...]
    rows = tm + 2 * H

    if seg is not None:
        pos = (lax.broadcasted_iota(jnp.int32, (rows, 1), 0) - H) % seg
        tap_prev = (pos != 0).astype(F32)
        tap_next = (pos != seg - 1).astype(F32)

    def conv(up, cols):
        w = cw_ref[:, cols]
        before, after = pltpu.roll(up, 1, 0), pltpu.roll(up, rows - 1, 0)
        if seg is not None:
            before, after = before * tap_prev, after * tap_next
        y = before * w[0:1] + up * w[1:2] + after * w[2:3]
        return y[H:H + tm]

    for j in range(D_FF // FF_CHUNK):
        cv = slice(j * FF_CHUNK, (j + 1) * FF_CHUNK)
        cg = slice(D_FF + j * FF_CHUNK, D_FF + (j + 1) * FF_CHUNK)
        val = conv(_dot(hf, wu_ref[:, cv]), cv)
        gate = conv(_dot(hf, wu_ref[:, cg]), cg)
        a_ref[:, cv] = (val * (gate * jax.nn.sigmoid(gate))).astype(BF16)
    y = x + gt_ref[...] * _dot(a_ref[...], wd_ref[...])
    if final:
        ms = jnp.mean(y * y, axis=-1, keepdims=True)
        y = y * lax.rsqrt(ms + EPS) * gf_ref[...]
    out_ref[...] = y


def _ffn(l, x, mod, gain, wu, cw, wd, gain_final, ctx_row, final, seg=None):
    B, T, _ = x.shape
    tm = min(FFN_TILE, T)
    assert seg is None or (tm == T and T % seg == 0)
    per = tm // SUBLANES
    nblk = T // SUBLANES
    tok = pl.BlockSpec((None, tm, D_MODEL), lambda b, i: (b, i, 0))
    prev = pl.BlockSpec((None, SUBLANES, D_MODEL), lambda b, i: (b, jnp.maximum(i * per - 1, 0), 0))
    nxt = pl.BlockSpec((None, SUBLANES, D_MODEL), lambda b, i: (b, jnp.minimum((i + 1) * per, nblk - 1), 0))
    return pl.pallas_call(
        functools.partial(_ffn_kernel, tm=tm, seg=seg, final=final),
        grid=(B, T // tm),
        in_specs=[
            tok, prev, nxt,
            _layer(gain, l),
            _mod_spec(l, 3, ctx_row), _mod_spec(l, 4, ctx_row), _mod_spec(l, 5, ctx_row),
            _layer(wu, l), _layer(cw, l), _layer(wd, l),
            _resident((1, D_MODEL)),
        ],
        out_specs=tok,
        out_shape=jax.ShapeDtypeStruct((B, T, D_MODEL), F32),
        scratch_shapes=[pltpu.VMEM((tm + 2 * SUBLANES, D_MODEL), BF16), pltpu.VMEM((tm, D_FF), BF16)],
        compiler_params=_params("arbitrary", "arbitrary"),
        name="ffn_final" if final else "ffn",
    )(x, x, x, gain, mod, mod, mod, wu, cw, wd, gain_final)


def _rope_tables(T):
    rows = T // GRID_W
    row = np.repeat(np.arange(rows), GRID_W).astype(np.float64)
    col = np.tile(np.arange(GRID_W), rows).astype(np.float64)
    n_freq = HEAD_DIM // 4
    inv_freq = ROPE_BASE ** (-np.arange(n_freq) / n_freq)
    ar, ac = row[:, None] * inv_freq[None, :], col[:, None] * inv_freq[None, :]
    cos = np.concatenate([np.cos(ar), np.cos(ar), np.cos(ac), np.cos(ac)], axis=1)
    sin = np.concatenate([-np.sin(ar), np.sin(ar), -np.sin(ac), np.sin(ac)], axis=1)
    reps = LANES // HEAD_DIM
    return jnp.asarray(np.tile(cos, (1, reps)), F32), jnp.asarray(np.tile(sin, (1, reps)), F32)


def kernel(x, c, ctx, c_ctx, w_mod, b_mod, norm_mix, norm_ffn, w_in, attn_sink, pool_w, pool_scale,
           w_br_attn, w_br_pool, w_br_four, w_out, w_up, conv_w, w_down, norm_final):
    B, T, _ = x.shape
    L = ctx.shape[1]
    depth = w_mod.shape[0]
    assert B < COND_ROWS and T % max(INPROJ_TILE, MERGE_TILE, FFN_TILE) == 0 and T % (ATTN_Q_BLOCKS * BLOCK) == 0
    assert L % BLOCK == 0

    cond = jnp.concatenate([c, c_ctx[None], jnp.zeros((COND_ROWS - B - 1, D_MODEL), F32)], axis=0)
    mod = _adaln(cond, w_mod, b_mod).reshape(depth, COND_ROWS, N_MOD, 1, D_MODEL)

    cos_t, sin_t = _rope_tables(T)
    cos_c = jnp.ones((B * L, LANES), F32)
    sin_c = jnp.zeros((B * L, LANES), F32)
    four_lat = _fourier_tables(T)
    four_ctx = _fourier_tables(L)

    w_a = w_in.astype(BF16)
    w_a_br = (w_br_attn.reshape(depth, N_KV_HEADS, Q_PER_KV, HEAD_DIM, D_MODEL)
              .transpose(0, 2, 1, 3, 4).reshape(depth, ATTN_WIDTH, D_MODEL).astype(BF16))
    w_p_br = w_br_pool.astype(BF16)
    w_f_br = w_br_four.astype(BF16)
    w_o = w_out.astype(BF16)
    groups = pool_w.shape[1]
    w_pool = jnp.einsum("lgab,gh->lgahb", pool_w, jnp.eye(groups, dtype=pool_w.dtype)).reshape(
        depth, POOL_WIDTH, POOL_WIDTH).astype(BF16)
    p_scale = pool_scale.reshape(depth, 1, POOL_WIDTH)
    w_u = w_up.astype(BF16)
    w_d = w_down.astype(BF16)
    g_mix = norm_mix.reshape(depth, 1, D_MODEL)
    g_ffn = norm_ffn.reshape(depth, 1, D_MODEL)
    gain_final = norm_final.reshape(1, D_MODEL)

    xc = ctx
    for l in range(depth):
        last = l == depth - 1
        k, v, q, u, f = _inproj(l, x, mod, g_mix, w_a, cos_t, sin_t, None)
        flat = lambda a: a.reshape((1, B * L) + a.shape[2:])
        kc, vc, qc, uc, fc = _inproj(l, flat(xc), mod, g_mix, w_a, cos_c, sin_c, B)
        kc, vc, qc, uc = (a.reshape((B, L) + a.shape[2:]) for a in (kc, vc, qc, uc))
        fc = fc.reshape(FOURIER_WIDTH // LANES, B, L, LANES).transpose(1, 0, 2, 3)
        o = _attention(l, q, k, v, kc, vc, attn_sink, True)
        x = _merge(l, x, o, _pool(l, u, w_pool, p_scale), _fourier(f, four_lat), mod, g_mix,
                   w_a, w_a_br, w_p_br, w_f_br, w_o, None)
        if not last:
            oc = _attention(l, qc, None, None, kc, vc, attn_sink, False)
            xc = _merge(l, flat(xc), flat(oc), flat(_pool(l, uc, w_pool, p_scale)), flat(_fourier(fc, four_ctx)),
                        mod, g_mix, w_a, w_a_br, w_p_br, w_f_br, w_o, B).reshape(B, L, D_MODEL)
        x = _ffn(l, x, mod, g_ffn, w_u, conv_w, w_d, gain_final, None, last)
        if not last:
            xc = _ffn(l, flat(xc), mod, g_ffn, w_u, conv_w, w_d, gain_final, B, False, seg=L).reshape(B, L, D_MODEL)
    return x
```

```python
import functools
import math

import numpy as np
import jax
import jax.numpy as jnp
from jax import lax
from jax.experimental import pallas as pl
from jax.experimental.pallas import tpu as pltpu

D_MODEL = 1024
N_Q_HEADS = 8
N_KV_HEADS = 2
HEAD_DIM = 64
Q_PER_KV = N_Q_HEADS // N_KV_HEADS
ATTN_WIDTH = N_Q_HEADS * HEAD_DIM
KV_WIDTH = N_KV_HEADS * HEAD_DIM
BLOCK = 128
GRID_W = 64
ROPE_BASE = 10000.0
POOL_WIDTH = 256
POOL_GROUP_DIM = 64
FOURIER_WIDTH = 256
FOURIER_GROUP_DIM = 64
D_FF = 2816
N_MOD = 6
EPS = 1e-6
NEG_INF = -1e30
LOG2E = math.log2(math.e)

LANES = 128
SUBLANES = 8
VMEM_LIMIT_BYTES = 56 * 1024 * 1024

INPROJ_TILE = 1024
TOKEN_TILE = 512
MERGE_TILE = 1024
FFN_TILE = 1024
ATTN_Q_BLOCKS = 8
ATTN_GROUP = 1
FF_CHUNK = 256
MERGE_CHUNK = 256
FOURIER_ROW_PAD = 8
FOURIER_UNROLL = 8
POOL_HALO = 8
ADALN_TILE = 1536
COND_ROWS = 8

F32 = jnp.float32
BF16 = jnp.bfloat16


def _dot(a, b):
    return jnp.dot(a, b, preferred_element_type=F32)


def _dot_nt(a, b):
    return lax.dot_general(a, b, (((1,), (1,)), ((), ())), preferred_element_type=F32)


def _modnorm(x, g, sc, sh):
    ms = jnp.mean(x * x, axis=-1, keepdims=True)
    y = x * lax.rsqrt(ms + EPS) * g
    return y * (1.0 + sc) + sh


def _resident(shape):
    nd = len(shape)
    return pl.BlockSpec(shape, lambda *_: (0,) * nd, pipeline_mode=pl.Buffered(1))


def _layer(arr, l):
    nd = arr.ndim - 1
    return pl.BlockSpec((None,) + arr.shape[1:], lambda *_: (l,) + (0,) * nd, pipeline_mode=pl.Buffered(1))


def _mod_spec(l, j, ctx_row):
    if ctx_row is None:
        return pl.BlockSpec((None, None, None, 1, D_MODEL), lambda b, i: (l, b, j, 0, 0))
    return pl.BlockSpec((None, None, None, 1, D_MODEL), lambda b, i: (l, ctx_row, j, 0, 0))


def _params(*sem):
    return pltpu.CompilerParams(dimension_semantics=sem, vmem_limit_bytes=VMEM_LIMIT_BYTES)


def _adaln_kernel(c_ref, w_ref, b_ref, o_ref):
    c = c_ref[...]
    a = (c * jax.nn.sigmoid(c)).astype(BF16)
    o_ref[...] = _dot(a, w_ref[...].astype(BF16)) + b_ref[...]


def _adaln(cond, w_mod, b_mod):
    depth, _, width = w_mod.shape
    return pl.pallas_call(
        _adaln_kernel,
        grid=(depth, width // ADALN_TILE),
        in_specs=[
            pl.BlockSpec((COND_ROWS, D_MODEL), lambda l, j: (0, 0)),
            pl.BlockSpec((None, D_MODEL, ADALN_TILE), lambda l, j: (l, 0, j)),
            pl.BlockSpec((None, 1, ADALN_TILE), lambda l, j: (l, 0, j)),
        ],
        out_specs=pl.BlockSpec((None, COND_ROWS, ADALN_TILE), lambda l, j: (l, 0, j)),
        out_shape=jax.ShapeDtypeStruct((depth, COND_ROWS, width), F32),
        compiler_params=_params("arbitrary", "arbitrary"),
        name="adaln",
    )(cond, w_mod, b_mod.reshape(depth, 1, width))


_K_COLS = 2 * KV_WIDTH
IN_TOKEN_COLS = 2 * KV_WIDTH + ATTN_WIDTH + POOL_WIDTH + FOURIER_WIDTH
_A_SPLITS = (0, KV_WIDTH, 2 * KV_WIDTH, 2 * KV_WIDTH + ATTN_WIDTH, 2 * KV_WIDTH + ATTN_WIDTH + POOL_WIDTH,
             IN_TOKEN_COLS)


def _inproj_kernel(x_ref, g_ref, sh_ref, sc_ref, w_ref, cos_ref, sin_ref,
                   k_ref, v_ref, q_ref, u_ref, f_ref):
    h = _modnorm(x_ref[...], g_ref[...], sc_ref[...], sh_ref[...]).astype(BF16)
    cos = cos_ref[...]
    sin = sin_ref[...]

    lane = lax.broadcasted_iota(jnp.int32, cos.shape, 1)
    first = (lane % (HEAD_DIM // 2)) < HEAD_DIM // 4
    low = lane < HEAD_DIM

    def rope(t):
        partner = jnp.where(first, pltpu.roll(t, LANES - HEAD_DIM // 4, 1), pltpu.roll(t, HEAD_DIM // 4, 1))
        return t * cos + partner * sin

    k0, v0, q0, u0, f0, end = _A_SPLITS
    tile = lambda a, j: a[:, j * LANES:(j + 1) * LANES]
    kvq = _dot(h, w_ref[:, k0:u0])
    k = rope(tile(kvq, 0))
    k_swapped = pltpu.roll(k, HEAD_DIM, 1)
    k_ref[:, 0:LANES] = jnp.where(low, k, k_swapped).astype(BF16)
    k_ref[:, LANES:2 * LANES] = jnp.where(low, k_swapped, k).astype(BF16)
    v_ref[...] = tile(kvq, v0 // LANES).astype(BF16)
    scale = HEAD_DIM ** -0.5 * LOG2E
    for j in range(ATTN_WIDTH // LANES):
        q_ref[:, j * LANES:(j + 1) * LANES] = (rope(tile(kvq, q0 // LANES + j)) * scale).astype(BF16)
    uf = _dot(h, w_ref[:, u0:end])
    u_ref[...] = uf[:, 0:f0 - u0]
    for j in range(FOURIER_WIDTH // LANES):
        f_ref[j] = tile(uf, (f0 - u0) // LANES + j)


def _inproj(l, x, mod, gain, w_a, cos_t, sin_t, ctx_row):
    B, T, _ = x.shape
    tm = min(INPROJ_TILE, T)
    tok = lambda w: pl.BlockSpec((None, tm, w), lambda b, i: (b, i, 0))
    return pl.pallas_call(
        _inproj_kernel,
        grid=(B, T // tm),
        in_specs=[
            tok(D_MODEL),
            _layer(gain, l),
            _mod_spec(l, 0, ctx_row), _mod_spec(l, 1, ctx_row),
            pl.BlockSpec((None, D_MODEL, IN_TOKEN_COLS), lambda b, i: (l, 0, 0), pipeline_mode=pl.Buffered(1)),
            pl.BlockSpec((tm, LANES), lambda b, i: (i, 0)),
            pl.BlockSpec((tm, LANES), lambda b, i: (i, 0)),
        ],
        out_specs=[tok(_K_COLS), tok(KV_WIDTH), tok(ATTN_WIDTH), tok(POOL_WIDTH),
                   pl.BlockSpec((None, FOURIER_WIDTH // LANES, tm, LANES), lambda b, i: (b, 0, i, 0))],
        out_shape=[
            jax.ShapeDtypeStruct((B, T, _K_COLS), BF16),
            jax.ShapeDtypeStruct((B, T, KV_WIDTH), BF16),
            jax.ShapeDtypeStruct((B, T, ATTN_WIDTH), BF16),
            jax.ShapeDtypeStruct((B, T, POOL_WIDTH), F32),
            jax.ShapeDtypeStruct((B, FOURIER_WIDTH // LANES, T, LANES), F32),
        ],
        compiler_params=_params("arbitrary", "arbitrary"),
        name="inproj",
    )(x, gain, mod, mod, w_a, cos_t, sin_t)


def _attn_kernel(sink_ref, q_ref, *refs, layer, steps, band):
    if band:
        kp_ref, km_ref, kn_ref, vp_ref, vm_ref, vn_ref, kc_ref, vc_ref, o_ref = refs
    else:
        kc_ref, vc_ref, o_ref = refs
    n = pl.program_id(1)
    rows = Q_PER_KV * BLOCK
    lane = lax.broadcasted_iota(jnp.int32, (BLOCK, LANES), 1)
    low = lane < HEAD_DIM
    rgrp = lax.broadcasted_iota(jnp.int32, (rows, 1), 0) // BLOCK
    if band:
        qi = lax.broadcasted_iota(jnp.int32, (rows, BLOCK), 0) % BLOCK
        kj = lax.broadcasted_iota(jnp.int32, (rows, BLOCK), 1)
        in_prev = kj >= qi
        in_next = kj <= qi
    blk = lambda ref, j, cols: ref[j * BLOCK:(j + 1) * BLOCK, cols]
    every = slice(None)

    def scores(q0, h, k_parts, ok_prev, ok_next):
        qs = []
        for t in range(Q_PER_KV // 2):
            c0 = (h * (Q_PER_KV // 2) + t) * LANES
            qt = q_ref[q0:q0 + BLOCK, c0:c0 + LANES]
            qs.append(jnp.where(low, qt, jnp.zeros_like(qt)))
            qs.append(jnp.where(low, jnp.zeros_like(qt), qt))
        qh = jnp.concatenate(qs, axis=0)
        s = _dot_nt(qh, jnp.concatenate(k_parts, axis=0))
        tiles = [s[:, j * LANES:(j + 1) * LANES] for j in range(s.shape[1] // LANES)]
        if band:
            tiles[0] = jnp.where(ok_prev, tiles[0], NEG_INF)
            tiles[2] = jnp.where(ok_next, tiles[2], NEG_INF)
        sink = jnp.zeros((rows, 1), F32)
        for g in range(Q_PER_KV):
            sink = jnp.where(rgrp == g, sink_ref[layer, h * Q_PER_KV + g] * LOG2E, sink)
        m_t = tiles[0]
        for t in tiles[1:]:
            m_t = jnp.maximum(m_t, t)
        m = jnp.maximum(jnp.max(m_t, axis=-1, keepdims=True), sink)
        return tiles, m, sink

    def weights(tiles, m, sink, v_all):
        p = jnp.concatenate([jnp.exp2((t - m).astype(BF16)) for t in tiles], axis=1)
        return _dot(p, v_all), jnp.exp2(sink - m)

    n_sub = ATTN_Q_BLOCKS if band else 1
    if band:
        pick = lambda pm, mid, nx, j, cols: (blk(pm, 0, cols) if j < 0 else
                                             blk(nx, 0, cols) if j >= n_sub else blk(mid, j, cols))

    def score_group(subs):
        staged = {}
        for sub in subs:
            if band:
                ok_prev = jnp.logical_and(in_prev, n > 0) if sub == 0 else in_prev
                ok_next = jnp.logical_and(in_next, n < steps - 1) if sub == n_sub - 1 else in_next
            else:
                ok_prev = ok_next = None
            for h in range(N_KV_HEADS):
                ks = slice(h * LANES, (h + 1) * LANES)
                k_parts = [kc_ref[:, ks]]
                if band:
                    k_parts = [pick(kp_ref, km_ref, kn_ref, sub + d, ks) for d in (-1, 0, 1)] + k_parts
                staged[sub, h] = scores(sub * BLOCK, h, k_parts, ok_prev, ok_next)
        return staged

    def value_group(subs, staged):
        for sub in subs:
            if band:
                v_all = jnp.concatenate([pick(vp_ref, vm_ref, vn_ref, sub + d, every) for d in (-1, 0, 1)]
                                        + [vc_ref[...]], axis=0)
            else:
                v_all = vc_ref[...]
            v_low = lax.broadcasted_iota(jnp.int32, v_all.shape, 1) < HEAD_DIM
            ones = jnp.ones_like(v_all)
            v_aug = [jnp.where(v_low, v_all, ones), jnp.where(v_low, ones, v_all)]
            (o0, e0), (o1, e1) = [weights(*staged[sub, h], v_aug[h]) for h in range(N_KV_HEADS)]
            for g in range(Q_PER_KV):
                r = slice(g * BLOCK, (g + 1) * BLOCK)
                num = jnp.where(low, o0[r], o1[r])
                den = pltpu.roll(jnp.where(low, o1[r], o0[r]), HEAD_DIM, 1) + jnp.where(low, e0[r], e1[r])
                o_ref[sub * BLOCK:(sub + 1) * BLOCK, g * LANES:(g + 1) * LANES] = (num / den).astype(BF16)

    size = ATTN_GROUP if band else 1
    pending = None
    for first in range(0, n_sub, size):
        subs = range(first, min(first + size, n_sub))
        staged = score_group(subs)
        if pending is not None:
            value_group(*pending)
        pending = (subs, staged)
    value_group(*pending)


def _attention(l, q, k, v, kc, vc, sink, band):
    B, T, _ = q.shape
    L = kc.shape[1]
    nq = ATTN_Q_BLOCKS if band else 1
    rows = nq * BLOCK
    steps = T // rows
    nb = T // BLOCK
    qspec = pl.BlockSpec((None, rows, ATTN_WIDTH), lambda b, n: (b, n, 0))
    kprev = lambda w: pl.BlockSpec((None, BLOCK, w), lambda b, n: (b, jnp.maximum(n * nq - 1, 0), 0))
    kmid = lambda w: pl.BlockSpec((None, rows, w), lambda b, n: (b, n, 0))
    knext = lambda w: pl.BlockSpec((None, BLOCK, w), lambda b, n: (b, jnp.minimum((n + 1) * nq, nb - 1), 0))
    cspec = lambda w: pl.BlockSpec((None, L, w), lambda b, n: (b, 0, 0))
    in_specs = [pl.BlockSpec(memory_space=pltpu.SMEM), qspec]
    args = [sink, q]
    if band:
        in_specs += [kprev(_K_COLS), kmid(_K_COLS), knext(_K_COLS),
                     kprev(KV_WIDTH), kmid(KV_WIDTH), knext(KV_WIDTH)]
        args += [k, k, k, v, v, v]
    in_specs += [cspec(_K_COLS), cspec(KV_WIDTH)]
    args += [kc, vc]
    return pl.pallas_call(
        functools.partial(_attn_kernel, layer=l, steps=steps, band=band),
        grid=(B, steps),
        in_specs=in_specs,
        out_specs=pl.BlockSpec((None, rows, ATTN_WIDTH), lambda b, n: (b, n, 0)),
        out_shape=jax.ShapeDtypeStruct((B, T, ATTN_WIDTH), BF16),
        compiler_params=_params("arbitrary", "arbitrary"),
        name="attention_band" if band else "attention_ctx",
    )(*args)


def _pool_kernel(u_ref, w_ref, s_ref, o_ref, pad_ref, *, T, tc):
    zeros = jnp.zeros((POOL_HALO, POOL_WIDTH), F32)
    pad_ref[0:POOL_HALO, :] = zeros
    pad_ref[POOL_HALO + T:POOL_HALO + T + POOL_HALO, :] = zeros
    pad_ref[POOL_HALO:POOL_HALO + T, :] = u_ref[...]
    lane = lax.broadcasted_iota(jnp.int32, (tc, LANES), 1)
    upper = lane >= POOL_GROUP_DIM
    row = lax.broadcasted_iota(jnp.int32, (tc, LANES), 0)
    for c in range(T // tc):
        base = POOL_HALO + c * tc
        t = row + c * tc
        pooled = []
        for tile in range(POOL_WIDTH // LANES):
            w_small = 2 << (2 * tile)
            w_big = 2 * w_small
            cols = slice(tile * LANES, (tile + 1) * LANES)
            load = lambda d: pad_ref[base + d:base + d + tc, cols]
            inner = load(-(w_small // 2))
            for d in range(-(w_small // 2) + 1, w_small // 2):
                inner = inner + load(d)
            outer = load(-(w_big // 2))
            for d in list(range(-(w_big // 2) + 1, -(w_small // 2))) + list(range(w_small // 2, w_big // 2)):
                outer = outer + load(d)
            win = inner + jnp.where(upper, outer, 0.0)
            if c * tc >= w_big // 2 and (c + 1) * tc + w_big // 2 <= T:
                mean = win * jnp.where(upper, 1.0 / w_big, 1.0 / w_small)
            else:
                half = jnp.where(upper, w_big // 2, w_small // 2)
                lo = jnp.maximum(t - half, 0)
                hi = jnp.minimum(t + half, T)
                mean = win / (hi - lo).astype(F32)
            pooled.append(mean - load(0))
        p = jnp.concatenate(pooled, axis=1).astype(BF16)
        y = _dot(p, w_ref[...]) * s_ref[...]
        o_ref[c * tc:(c + 1) * tc, :] = y.astype(BF16)


def _pool(l, u, w_bd, scale):
    B, T, _ = u.shape
    tc = min(TOKEN_TILE, T)
    return pl.pallas_call(
        functools.partial(_pool_kernel, T=T, tc=tc),
        grid=(B,),
        in_specs=[
            pl.BlockSpec((None, T, POOL_WIDTH), lambda b: (b, 0, 0)),
            _layer(w_bd, l),
            _layer(scale, l),
        ],
        out_specs=pl.BlockSpec((None, T, POOL_WIDTH), lambda b: (b, 0, 0)),
        out_shape=jax.ShapeDtypeStruct((B, T, POOL_WIDTH), BF16),
        scratch_shapes=[pltpu.VMEM((T + 2 * POOL_HALO, POOL_WIDTH), F32)],
        compiler_params=_params("arbitrary"),
        name="pool",
    )(u, w_bd, scale)


def _fourier_factors(T):
    n1 = 1 << (int(math.log2(T)) // 2)
    return n1, T // n1


def _split_const(a):
    a = jnp.asarray(a, F32)
    hi = a.astype(BF16)
    return hi, (a - hi.astype(F32)).astype(BF16)


def _fourier_tables(T):
    n1, n2 = _fourier_factors(T)
    gd = FOURIER_GROUP_DIM
    t2 = np.arange(n2)[:, None, None]
    k1 = np.arange(n1)[None, :, None]
    t1 = np.arange(n1)[None, None, :]
    ph = -2.0 * np.pi * (t2 * k1 / T + t1 * k1 / n1)
    m1 = np.concatenate([np.cos(ph), np.sin(ph)], axis=1) / math.sqrt(n1)
    a3 = 2.0 * np.pi * np.outer(np.arange(n2), np.arange(n2)) / n2
    c3, s3 = np.cos(a3) / math.sqrt(n2), np.sin(a3) / math.sqrt(n2)
    m3 = np.block([[c3, s3], [-s3, c3]])
    c = np.arange(gd)
    ang = 2.0 * np.pi * np.outer(c, c) / gd
    eye = np.eye(FOURIER_WIDTH // gd)
    cd = np.concatenate([np.kron(eye, np.cos(ang)), np.kron(eye, np.sin(ang))], axis=0) / math.sqrt(gd)
    return _split_const(m1) + _split_const(m3) + _split_const(cd)


def _split(a):
    hi = a.astype(BF16)
    return hi, (a - hi.astype(F32)).astype(BF16)


def _dot3(a, b):
    return _dot(a[0], b[0]) + (_dot(a[1], b[0]) + _dot(a[0], b[1]))


def _fourier_kernel(f_ref, m1h_ref, m1l_ref, m3h_ref, m3l_ref, cdh_ref, cdl_ref, o_ref, a_ref, g_ref,
                    *, n1, n2, tc, unroll):
    T = n1 * n2
    wt = FOURIER_WIDTH // LANES
    lanes = lambda j: slice(j * LANES, (j + 1) * LANES)
    pitch = n1 + FOURIER_ROW_PAD

    def over_t1(t2, carry):
        rows = pl.ds(t2, n1, stride=n2)
        x = jnp.concatenate([f_ref[j, rows, :] for j in range(wt)], axis=1)
        a = _dot3((m1h_ref[t2], m1l_ref[t2]), _split(x))
        off = pl.multiple_of(t2 * pitch, SUBLANES)
        for j in range(wt):
            a_ref[j, pl.ds(off, n1), :] = a[:n1, lanes(j)]
            a_ref[wt + j, pl.ds(off, n1), :] = a[n1:, lanes(j)]
        return carry

    lax.fori_loop(0, n2, over_t1, 0, unroll=unroll)

    def over_t2(k1, carry):
        rows = pl.ds(k1, n2, stride=pitch)
        b = jnp.concatenate(
            [jnp.concatenate([a_ref[h * wt + j, rows, :] for j in range(wt)], axis=1) for h in range(2)], axis=0)
        g = _dot3((m3h_ref[...], m3l_ref[...]), _split(b))
        for j in range(wt):
            g_ref[j, rows, :] = g[:n2, lanes(j)]
            g_ref[wt + j, rows, :] = g[n2:, lanes(j)]
        return carry

    lax.fori_loop(0, n1, over_t2, 0, unroll=unroll)

    per = tc // n1
    for c in range(T // tc):
        groups = [slice((c * per + i) * pitch, (c * per + i) * pitch + n1) for i in range(per)]
        g = jnp.concatenate([jnp.concatenate([g_ref[j, r, :] for r in groups], axis=0) for j in range(2 * wt)],
                            axis=1)
        o_ref[c * tc:(c + 1) * tc, :] = _dot3(_split(g), (cdh_ref[...], cdl_ref[...])).astype(BF16)


def _fourier(f, tables):
    B, wt, T, _ = f.shape
    n1, n2 = _fourier_factors(T)
    return pl.pallas_call(
        functools.partial(_fourier_kernel, n1=n1, n2=n2, tc=min(TOKEN_TILE, T), unroll=FOURIER_UNROLL),
        grid=(B,),
        in_specs=[pl.BlockSpec((None, wt, T, LANES), lambda b: (b, 0, 0, 0))] + [_resident(t.shape) for t in tables],
        out_specs=pl.BlockSpec((None, T, FOURIER_WIDTH), lambda b: (b, 0, 0)),
        out_shape=jax.ShapeDtypeStruct((B, T, FOURIER_WIDTH), BF16),
        scratch_shapes=[pltpu.VMEM((2 * wt, n2 * (n1 + FOURIER_ROW_PAD), LANES), F32)] * 2,
        compiler_params=_params("arbitrary"),
        name="fourier",
    )(f, *tables)


def _merge_kernel(x_ref, o_ref, p_ref, f_ref, g_ref, sh_ref, sc_ref, gt_ref,
                  wg_ref, wa_ref, wp_ref, wf_ref, wo_ref, out_ref, y_ref):
    x = x_ref[...]
    h = _modnorm(x, g_ref[...], sc_ref[...], sh_ref[...]).astype(BF16)
    o = o_ref[...]
    p = p_ref[...]
    f = f_ref[...]
    for j in range(D_MODEL // MERGE_CHUNK):
        c = slice(j * MERGE_CHUNK, (j + 1) * MERGE_CHUNK)
        gcol = lambda b: IN_TOKEN_COLS + b * D_MODEL + j * MERGE_CHUNK
        gate = lambda b: jax.nn.sigmoid(_dot(h, wg_ref[:, gcol(b):gcol(b) + MERGE_CHUNK]))
        y = (gate(0) * _dot(o, wa_ref[:, c]) + gate(1) * _dot(p, wp_ref[:, c])
             + gate(2) * _dot(f, wf_ref[:, c]))
        y_ref[:, c] = y.astype(BF16)
    out_ref[...] = x + gt_ref[...] * _dot(y_ref[...], wo_ref[...])


def _merge(l, x, o, p, f, mod, gain, wg, wa, wp, wf, wo, ctx_row):
    B, T, _ = x.shape
    tm = min(MERGE_TILE, T)
    tok = lambda w: pl.BlockSpec((None, tm, w), lambda b, i: (b, i, 0))
    return pl.pallas_call(
        _merge_kernel,
        grid=(B, T // tm),
        in_specs=[
            tok(D_MODEL), tok(ATTN_WIDTH), tok(POOL_WIDTH), tok(FOURIER_WIDTH),
            _layer(gain, l),
            _mod_spec(l, 0, ctx_row), _mod_spec(l, 1, ctx_row), _mod_spec(l, 2, ctx_row),
            _layer(wg, l), _layer(wa, l), _layer(wp, l), _layer(wf, l), _layer(wo, l),
        ],
        out_specs=tok(D_MODEL),
        out_shape=jax.ShapeDtypeStruct((B, T, D_MODEL), F32),
        scratch_shapes=[pltpu.VMEM((tm, D_MODEL), BF16)],
        compiler_params=_params("arbitrary", "arbitrary"),
        name="merge",
    )(x, o, p, f, gain, mod, mod, mod, wg, wa, wp, wf, wo)


def _ffn_kernel(x_ref, xp_ref, xn_ref, g_ref, sh_ref, sc_ref, gt_ref, wu_ref, cw_ref, wd_ref, gf_ref,
                out_ref, h_ref, a_ref, *, tm, seg, final):
    i = pl.program_id(1)
    last = pl.num_programs(1) - 1
    g, sh, sc = g_ref[...], sh_ref[...], sc_ref[...]
    x = x_ref[...]
    H = SUBLANES
    h_ref[H:H + tm, :] = _modnorm(x, g, sc, sh).astype(BF16)
    keep_prev = (i > 0).astype(F32)
    keep_next = (i < last).astype(F32)
    h_ref[0:H, :] = (_modnorm(xp_ref[...], g, sc, sh) * keep_prev).astype(BF16)
    h_ref[H + tm:H + tm + H, :] = (_modnorm(xn_ref[...], g, sc, sh) * keep_next).astype(BF16)
    hf = h_ref[...]
    rows = tm + 2 * H

    if seg is not None:
        pos = (lax.broadcasted_iota(jnp.int32, (rows, 1), 0) - H) % seg
        tap_prev = (pos != 0).astype(F32)
        tap_next = (pos != seg - 1).astype(F32)

    def conv(up, cols):
        w = cw_ref[:, cols]
        before, after = pltpu.roll(up, 1, 0), pltpu.roll(up, rows - 1, 0)
        if seg is not None:
            before, after = before * tap_prev, after * tap_next
        y = before * w[0:1] + up * w[1:2] + after * w[2:3]
        return y[H:H + tm]

    for j in range(D_FF // FF_CHUNK):
        cv = slice(j * FF_CHUNK, (j + 1) * FF_CHUNK)
        cg = slice(D_FF + j * FF_CHUNK, D_FF + (j + 1) * FF_CHUNK)
        val = conv(_dot(hf, wu_ref[:, cv]), cv)
        gate = conv(_dot(hf, wu_ref[:, cg]), cg)
        a_ref[:, cv] = (val * (gate * jax.nn.sigmoid(gate))).astype(BF16)
    y = x + gt_ref[...] * _dot(a_ref[...], wd_ref[...])
    if final:
        ms = jnp.mean(y * y, axis=-1, keepdims=True)
        y = y * lax.rsqrt(ms + EPS) * gf_ref[...]
    out_ref[...] = y


def _ffn(l, x, mod, gain, wu, cw, wd, gain_final, ctx_row, final, seg=None):
    B, T, _ = x.shape
    tm = min(FFN_TILE, T)
    assert seg is None or (tm == T and T % seg == 0)
    per = tm // SUBLANES
    nblk = T // SUBLANES
    tok = pl.BlockSpec((None, tm, D_MODEL), lambda b, i: (b, i, 0))
    prev = pl.BlockSpec((None, SUBLANES, D_MODEL), lambda b, i: (b, jnp.maximum(i * per - 1, 0), 0))
    nxt = pl.BlockSpec((None, SUBLANES, D_MODEL), lambda b, i: (b, jnp.minimum((i + 1) * per, nblk - 1), 0))
    return pl.pallas_call(
        functools.partial(_ffn_kernel, tm=tm, seg=seg, final=final),
        grid=(B, T // tm),
        in_specs=[
            tok, prev, nxt,
            _layer(gain, l),
            _mod_spec(l, 3, ctx_row), _mod_spec(l, 4, ctx_row), _mod_spec(l, 5, ctx_row),
            _layer(wu, l), _layer(cw, l), _layer(wd, l),
            _resident((1, D_MODEL)),
        ],
        out_specs=tok,
        out_shape=jax.ShapeDtypeStruct((B, T, D_MODEL), F32),
        scratch_shapes=[pltpu.VMEM((tm + 2 * SUBLANES, D_MODEL), BF16), pltpu.VMEM((tm, D_FF), BF16)],
        compiler_params=_params("arbitrary", "arbitrary"),
        name="ffn_final" if final else "ffn",
    )(x, x, x, gain, mod, mod, mod, wu, cw, wd, gain_final)


def _rope_tables(T):
    rows = T // GRID_W
    row = np.repeat(np.arange(rows), GRID_W).astype(np.float64)
    col = np.tile(np.arange(GRID_W), rows).astype(np.float64)
    n_freq = HEAD_DIM // 4
    inv_freq = ROPE_BASE ** (-np.arange(n_freq) / n_freq)
    ar, ac = row[:, None] * inv_freq[None, :], col[:, None] * inv_freq[None, :]
    cos = np.concatenate([np.cos(ar), np.cos(ar), np.cos(ac), np.cos(ac)], axis=1)
    sin = np.concatenate([-np.sin(ar), np.sin(ar), -np.sin(ac), np.sin(ac)], axis=1)
    reps = LANES // HEAD_DIM
    return jnp.asarray(np.tile(cos, (1, reps)), F32), jnp.asarray(np.tile(sin, (1, reps)), F32)


def kernel(x, c, ctx, c_ctx, w_mod, b_mod, norm_mix, norm_ffn, w_in, attn_sink, pool_w, pool_scale,
           w_br_attn, w_br_pool, w_br_four, w_out, w_up, conv_w, w_down, norm_final):
    B, T, _ = x.shape
    L = ctx.shape[1]
    depth = w_mod.shape[0]
    assert B < COND_ROWS and T % max(INPROJ_TILE, MERGE_TILE, FFN_TILE) == 0 and T % (ATTN_Q_BLOCKS * BLOCK) == 0
    assert L % BLOCK == 0

    cond = jnp.concatenate([c, c_ctx[None], jnp.zeros((COND_ROWS - B - 1, D_MODEL), F32)], axis=0)
    mod = _adaln(cond, w_mod, b_mod).reshape(depth, COND_ROWS, N_MOD, 1, D_MODEL)

    cos_t, sin_t = _rope_tables(T)
    cos_c = jnp.ones((B * L, LANES), F32)
    sin_c = jnp.zeros((B * L, LANES), F32)
    four_lat = _fourier_tables(T)
    four_ctx = _fourier_tables(L)

    w_a = w_in.astype(BF16)
    w_a_br = (w_br_attn.reshape(depth, N_KV_HEADS, Q_PER_KV, HEAD_DIM, D_MODEL)
              .transpose(0, 2, 1, 3, 4).reshape(depth, ATTN_WIDTH, D_MODEL).astype(BF16))
    w_p_br = w_br_pool.astype(BF16)
    w_f_br = w_br_four.astype(BF16)
    w_o = w_out.astype(BF16)
    groups = pool_w.shape[1]
    w_pool = jnp.einsum("lgab,gh->lgahb", pool_w, jnp.eye(groups, dtype=pool_w.dtype)).reshape(
        depth, POOL_WIDTH, POOL_WIDTH).astype(BF16)
    p_scale = pool_scale.reshape(depth, 1, POOL_WIDTH)
    w_u = w_up.astype(BF16)
    w_d = w_down.astype(BF16)
    g_mix = norm_mix.reshape(depth, 1, D_MODEL)
    g_ffn = norm_ffn.reshape(depth, 1, D_MODEL)
    gain_final = norm_final.reshape(1, D_MODEL)

    xc = ctx
    for l in range(depth):
        last = l == depth - 1
        k, v, q, u, f = _inproj(l, x, mod, g_mix, w_a, cos_t, sin_t, None)
        flat = lambda a: a.reshape((1, B * L) + a.shape[2:])
        kc, vc, qc, uc, fc = _inproj(l, flat(xc), mod, g_mix, w_a, cos_c, sin_c, B)
        kc, vc, qc, uc = (a.reshape((B, L) + a.shape[2:]) for a in (kc, vc, qc, uc))
        fc = fc.reshape(FOURIER_WIDTH // LANES, B, L, LANES).transpose(1, 0, 2, 3)
        o = _attention(l, q, k, v, kc, vc, attn_sink, True)
        x = _merge(l, x, o, _pool(l, u, w_pool, p_scale), _fourier(f, four_lat), mod, g_mix,
                   w_a, w_a_br, w_p_br, w_f_br, w_o, None)
        if not last:
            oc = _attention(l, qc, None, None, kc, vc, attn_sink, False)
            xc = _merge(l, flat(xc), flat(oc), flat(_pool(l, uc, w_pool, p_scale)), flat(_fourier(fc, four_ctx)),
                        mod, g_mix, w_a, w_a_br, w_p_br, w_f_br, w_o, B).reshape(B, L, D_MODEL)
        x = _ffn(l, x, mod, g_ffn, w_u, conv_w, w_d, gain_final, None, last)
        if not last:
            xc = _ffn(l, flat(xc), mod, g_ffn, w_u, conv_w, w_d, gain_final, B, False, seg=L).reshape(B, L, D_MODEL)
    return x
```

```python
import functools
import math

import numpy as np
import jax
import jax.numpy as jnp
from jax import lax
from jax.experimental import pallas as pl
from jax.experimental.pallas import tpu as pltpu

D_MODEL = 1024
N_Q_HEADS = 8
N_KV_HEADS = 2
HEAD_DIM = 64
Q_PER_KV = N_Q_HEADS // N_KV_HEADS
ATTN_WIDTH = N_Q_HEADS * HEAD_DIM
KV_WIDTH = N_KV_HEADS * HEAD_DIM
BLOCK = 128
GRID_W = 64
ROPE_BASE = 10000.0
POOL_WIDTH = 256
POOL_GROUP_DIM = 64
FOURIER_WIDTH = 256
FOURIER_GROUP_DIM = 64
D_FF = 2816
N_MOD = 6
EPS = 1e-6
NEG_INF = -1e30
LOG2E = math.log2(math.e)

LANES = 128
SUBLANES = 8
VMEM_LIMIT_BYTES = 56 * 1024 * 1024

INPROJ_TILE = 1024
TOKEN_TILE = 512
MERGE_TILE = 1024
FFN_TILE = 1024
ATTN_Q_BLOCKS = 8
ATTN_GROUP = 1
FF_CHUNK = 256
MERGE_CHUNK = 256
FOURIER_ROW_PAD = 8
FOURIER_UNROLL = 8
POOL_HALO = 8
ADALN_TILE = 1536
COND_ROWS = 8

F32 = jnp.float32
BF16 = jnp.bfloat16


def _dot(a, b):
    return jnp.dot(a, b, preferred_element_type=F32)


def _dot_nt(a, b):
    return lax.dot_general(a, b, (((1,), (1,)), ((), ())), preferred_element_type=F32)


def _modnorm(x, g, sc, sh):
    ms = jnp.mean(x * x, axis=-1, keepdims=True)
    return (x * lax.rsqrt(ms + EPS)) * (g * (1.0 + sc)) + sh


def _resident(shape):
    nd = len(shape)
    return pl.BlockSpec(shape, lambda *_: (0,) * nd, pipeline_mode=pl.Buffered(1))


def _layer(arr, l):
    nd = arr.ndim - 1
    return pl.BlockSpec((None,) + arr.shape[1:], lambda *_: (l,) + (0,) * nd, pipeline_mode=pl.Buffered(1))


def _mod_spec(l, j, ctx_row):
    if ctx_row is None:
        return pl.BlockSpec((None, None, None, 1, D_MODEL), lambda b, i: (l, b, j, 0, 0))
    return pl.BlockSpec((None, None, None, 1, D_MODEL), lambda b, i: (l, ctx_row, j, 0, 0))


def _params(*sem):
    return pltpu.CompilerParams(dimension_semantics=sem, vmem_limit_bytes=VMEM_LIMIT_BYTES)


def _adaln_kernel(c_ref, w_ref, b_ref, o_ref):
    c = c_ref[...]
    a = (c * jax.nn.sigmoid(c)).astype(BF16)
    o_ref[...] = _dot(a, w_ref[...].astype(BF16)) + b_ref[...]


def _adaln(cond, w_mod, b_mod):
    depth, _, width = w_mod.shape
    return pl.pallas_call(
        _adaln_kernel,
        grid=(depth, width // ADALN_TILE),
        in_specs=[
            pl.BlockSpec((COND_ROWS, D_MODEL), lambda l, j: (0, 0)),
            pl.BlockSpec((None, D_MODEL, ADALN_TILE), lambda l, j: (l, 0, j)),
            pl.BlockSpec((None, 1, ADALN_TILE), lambda l, j: (l, 0, j)),
        ],
        out_specs=pl.BlockSpec((None, COND_ROWS, ADALN_TILE), lambda l, j: (l, 0, j)),
        out_shape=jax.ShapeDtypeStruct((depth, COND_ROWS, width), F32),
        compiler_params=_params("arbitrary", "arbitrary"),
        name="adaln",
    )(cond, w_mod, b_mod.reshape(depth, 1, width))


_K_COLS = 2 * KV_WIDTH
IN_TOKEN_COLS = 2 * KV_WIDTH + ATTN_WIDTH + POOL_WIDTH + FOURIER_WIDTH
_A_SPLITS = (0, KV_WIDTH, 2 * KV_WIDTH, 2 * KV_WIDTH + ATTN_WIDTH, 2 * KV_WIDTH + ATTN_WIDTH + POOL_WIDTH,
             IN_TOKEN_COLS)


def _inproj_kernel(x_ref, g_ref, sh_ref, sc_ref, w_ref, cos_ref, sin_ref,
                   k_ref, v_ref, q_ref, u_ref, f_ref):
    h = _modnorm(x_ref[...], g_ref[...], sc_ref[...], sh_ref[...]).astype(BF16)
    cos = cos_ref[...]
    sin = sin_ref[...]

    lane = lax.broadcasted_iota(jnp.int32, cos.shape, 1)
    first = (lane % (HEAD_DIM // 2)) < HEAD_DIM // 4
    low = lane < HEAD_DIM

    def rope(t):
        partner = jnp.where(first, pltpu.roll(t, LANES - HEAD_DIM // 4, 1), pltpu.roll(t, HEAD_DIM // 4, 1))
        return t * cos + partner * sin

    k0, v0, q0, u0, f0, end = _A_SPLITS
    tile = lambda a, j: a[:, j * LANES:(j + 1) * LANES]
    kvq = _dot(h, w_ref[:, k0:u0])
    k = rope(tile(kvq, 0))
    k_swapped = pltpu.roll(k, HEAD_DIM, 1)
    k_ref[:, 0:LANES] = jnp.where(low, k, k_swapped).astype(BF16)
    k_ref[:, LANES:2 * LANES] = jnp.where(low, k_swapped, k).astype(BF16)
    v_ref[...] = tile(kvq, v0 // LANES).astype(BF16)
    scale = HEAD_DIM ** -0.5 * LOG2E
    for j in range(ATTN_WIDTH // LANES):
        q_ref[:, j * LANES:(j + 1) * LANES] = (rope(tile(kvq, q0 // LANES + j)) * scale).astype(BF16)
    uf = _dot(h, w_ref[:, u0:end])
    u_ref[...] = uf[:, 0:f0 - u0]
    for j in range(FOURIER_WIDTH // LANES):
        f_ref[j] = tile(uf, (f0 - u0) // LANES + j)


def _inproj(l, x, mod, gain, w_a, cos_t, sin_t, ctx_row):
    B, T, _ = x.shape
    tm = min(INPROJ_TILE, T)
    tok = lambda w: pl.BlockSpec((None, tm, w), lambda b, i: (b, i, 0))
    return pl.pallas_call(
        _inproj_kernel,
        grid=(B, T // tm),
        in_specs=[
            tok(D_MODEL),
            _layer(gain, l),
            _mod_spec(l, 0, ctx_row), _mod_spec(l, 1, ctx_row),
            pl.BlockSpec((None, D_MODEL, IN_TOKEN_COLS), lambda b, i: (l, 0, 0), pipeline_mode=pl.Buffered(1)),
            pl.BlockSpec((tm, LANES), lambda b, i: (i, 0)),
            pl.BlockSpec((tm, LANES), lambda b, i: (i, 0)),
        ],
        out_specs=[tok(_K_COLS), tok(KV_WIDTH), tok(ATTN_WIDTH), tok(POOL_WIDTH),
                   pl.BlockSpec((None, FOURIER_WIDTH // LANES, tm, LANES), lambda b, i: (b, 0, i, 0))],
        out_shape=[
            jax.ShapeDtypeStruct((B, T, _K_COLS), BF16),
            jax.ShapeDtypeStruct((B, T, KV_WIDTH), BF16),
            jax.ShapeDtypeStruct((B, T, ATTN_WIDTH), BF16),
            jax.ShapeDtypeStruct((B, T, POOL_WIDTH), F32),
            jax.ShapeDtypeStruct((B, FOURIER_WIDTH // LANES, T, LANES), F32),
        ],
        compiler_params=_params("arbitrary", "arbitrary"),
        name="inproj",
    )(x, gain, mod, mod, w_a, cos_t, sin_t)


def _attn_kernel(sink_ref, q_ref, *refs, layer, steps, band):
    if band:
        kp_ref, km_ref, kn_ref, vp_ref, vm_ref, vn_ref, kc_ref, vc_ref, o_ref = refs
    else:
        kc_ref, vc_ref, o_ref = refs
    n = pl.program_id(1)
    rows = Q_PER_KV * BLOCK
    lane = lax.broadcasted_iota(jnp.int32, (BLOCK, LANES), 1)
    low = lane < HEAD_DIM
    rgrp = lax.broadcasted_iota(jnp.int32, (rows, 1), 0) // BLOCK
    if band:
        qi = lax.broadcasted_iota(jnp.int32, (rows, BLOCK), 0) % BLOCK
        kj = lax.broadcasted_iota(jnp.int32, (rows, BLOCK), 1)
        in_prev = kj >= qi
        in_next = kj <= qi
    blk = lambda ref, j, cols: ref[j * BLOCK:(j + 1) * BLOCK, cols]
    every = slice(None)

    def scores(q0, h, k_parts, ok_prev, ok_next):
        qs = []
        for t in range(Q_PER_KV // 2):
            c0 = (h * (Q_PER_KV // 2) + t) * LANES
            qt = q_ref[q0:q0 + BLOCK, c0:c0 + LANES]
            qs.append(jnp.where(low, qt, jnp.zeros_like(qt)))
            qs.append(jnp.where(low, jnp.zeros_like(qt), qt))
        qh = jnp.concatenate(qs, axis=0)
        s = _dot_nt(qh, jnp.concatenate(k_parts, axis=0))
        tiles = [s[:, j * LANES:(j + 1) * LANES] for j in range(s.shape[1] // LANES)]
        if band:
            tiles[0] = jnp.where(ok_prev, tiles[0], NEG_INF)
            tiles[2] = jnp.where(ok_next, tiles[2], NEG_INF)
        sink = jnp.zeros((rows, 1), F32)
        for g in range(Q_PER_KV):
            sink = jnp.where(rgrp == g, sink_ref[layer, h * Q_PER_KV + g] * LOG2E, sink)
        m_t = tiles[0]
        for t in tiles[1:]:
            m_t = jnp.maximum(m_t, t)
        m = jnp.maximum(jnp.max(m_t, axis=-1, keepdims=True), sink)
        return tiles, m, sink

    def weights(tiles, m, sink, v_all):
        p = jnp.concatenate([jnp.exp2((t - m).astype(BF16)) for t in tiles], axis=1)
        return _dot(p, v_all), jnp.exp2(sink - m)

    n_sub = ATTN_Q_BLOCKS if band else 1
    if band:
        pick = lambda pm, mid, nx, j, cols: (blk(pm, 0, cols) if j < 0 else
                                             blk(nx, 0, cols) if j >= n_sub else blk(mid, j, cols))

    def score_group(subs):
        staged = {}
        for sub in subs:
            if band:
                ok_prev = jnp.logical_and(in_prev, n > 0) if sub == 0 else in_prev
                ok_next = jnp.logical_and(in_next, n < steps - 1) if sub == n_sub - 1 else in_next
            else:
                ok_prev = ok_next = None
            for h in range(N_KV_HEADS):
                ks = slice(h * LANES, (h + 1) * LANES)
                k_parts = [kc_ref[:, ks]]
                if band:
                    k_parts = [pick(kp_ref, km_ref, kn_ref, sub + d, ks) for d in (-1, 0, 1)] + k_parts
                staged[sub, h] = scores(sub * BLOCK, h, k_parts, ok_prev, ok_next)
        return staged

    def value_group(subs, staged):
        for sub in subs:
            if band:
                v_all = jnp.concatenate([pick(vp_ref, vm_ref, vn_ref, sub + d, every) for d in (-1, 0, 1)]
                                        + [vc_ref[...]], axis=0)
            else:
                v_all = vc_ref[...]
            v_low = lax.broadcasted_iota(jnp.int32, v_all.shape, 1) < HEAD_DIM
            ones = jnp.ones_like(v_all)
            v_aug = [jnp.where(v_low, v_all, ones), jnp.where(v_low, ones, v_all)]
            (o0, e0), (o1, e1) = [weights(*staged[sub, h], v_aug[h]) for h in range(N_KV_HEADS)]
            for g in range(Q_PER_KV):
                r = slice(g * BLOCK, (g + 1) * BLOCK)
                num = jnp.where(low, o0[r], o1[r])
                den = pltpu.roll(jnp.where(low, o1[r], o0[r]), HEAD_DIM, 1) + jnp.where(low, e0[r], e1[r])
                o_ref[sub * BLOCK:(sub + 1) * BLOCK, g * LANES:(g + 1) * LANES] = (num / den).astype(BF16)

    size = ATTN_GROUP if band else 1
    pending = None
    for first in range(0, n_sub, size):
        subs = range(first, min(first + size, n_sub))
        staged = score_group(subs)
        if pending is not None:
            value_group(*pending)
        pending = (subs, staged)
    value_group(*pending)


def _attention(l, q, k, v, kc, vc, sink, band):
    B, T, _ = q.shape
    L = kc.shape[1]
    nq = ATTN_Q_BLOCKS if band else 1
    rows = nq * BLOCK
    steps = T // rows
    nb = T // BLOCK
    qspec = pl.BlockSpec((None, rows, ATTN_WIDTH), lambda b, n: (b, n, 0))
    kprev = lambda w: pl.BlockSpec((None, BLOCK, w), lambda b, n: (b, jnp.maximum(n * nq - 1, 0), 0))
    kmid = lambda w: pl.BlockSpec((None, rows, w), lambda b, n: (b, n, 0))
    knext = lambda w: pl.BlockSpec((None, BLOCK, w), lambda b, n: (b, jnp.minimum((n + 1) * nq, nb - 1), 0))
    cspec = lambda w: pl.BlockSpec((None, L, w), lambda b, n: (b, 0, 0))
    in_specs = [pl.BlockSpec(memory_space=pltpu.SMEM), qspec]
    args = [sink, q]
    if band:
        in_specs += [kprev(_K_COLS), kmid(_K_COLS), knext(_K_COLS),
                     kprev(KV_WIDTH), kmid(KV_WIDTH), knext(KV_WIDTH)]
        args += [k, k, k, v, v, v]
    in_specs += [cspec(_K_COLS), cspec(KV_WIDTH)]
    args += [kc, vc]
    return pl.pallas_call(
        functools.partial(_attn_kernel, layer=l, steps=steps, band=band),
        grid=(B, steps),
        in_specs=in_specs,
        out_specs=pl.BlockSpec((None, rows, ATTN_WIDTH), lambda b, n: (b, n, 0)),
        out_shape=jax.ShapeDtypeStruct((B, T, ATTN_WIDTH), BF16),
        compiler_params=_params("arbitrary", "arbitrary"),
        name="attention_band" if band else "attention_ctx",
    )(*args)


def _pool_kernel(u_ref, w_ref, s_ref, o_ref, pad_ref, *, T, tc):
    zeros = jnp.zeros((POOL_HALO, POOL_WIDTH), F32)
    pad_ref[0:POOL_HALO, :] = zeros
    pad_ref[POOL_HALO + T:POOL_HALO + T + POOL_HALO, :] = zeros
    pad_ref[POOL_HALO:POOL_HALO + T, :] = u_ref[...]
    lane = lax.broadcasted_iota(jnp.int32, (tc, LANES), 1)
    upper = lane >= POOL_GROUP_DIM
    row = lax.broadcasted_iota(jnp.int32, (tc, LANES), 0)
    for c in range(T // tc):
        base = POOL_HALO + c * tc
        t = row + c * tc
        pooled = []
        for tile in range(POOL_WIDTH // LANES):
            w_small = 2 << (2 * tile)
            w_big = 2 * w_small
            cols = slice(tile * LANES, (tile + 1) * LANES)
            load = lambda d: pad_ref[base + d:base + d + tc, cols]
            inner = load(-(w_small // 2))
            for d in range(-(w_small // 2) + 1, w_small // 2):
                inner = inner + load(d)
            outer = load(-(w_big // 2))
            for d in list(range(-(w_big // 2) + 1, -(w_small // 2))) + list(range(w_small // 2, w_big // 2)):
                outer = outer + load(d)
            win = inner + jnp.where(upper, outer, 0.0)
            if c * tc >= w_big // 2 and (c + 1) * tc + w_big // 2 <= T:
                mean = win * jnp.where(upper, 1.0 / w_big, 1.0 / w_small)
            else:
                half = jnp.where(upper, w_big // 2, w_small // 2)
                lo = jnp.maximum(t - half, 0)
                hi = jnp.minimum(t + half, T)
                mean = win / (hi - lo).astype(F32)
            pooled.append(mean - load(0))
        p = jnp.concatenate(pooled, axis=1).astype(BF16)
        y = _dot(p, w_ref[...]) * s_ref[...]
        o_ref[c * tc:(c + 1) * tc, :] = y.astype(BF16)


def _pool(l, u, w_bd, scale):
    B, T, _ = u.shape
    tc = min(TOKEN_TILE, T)
    return pl.pallas_call(
        functools.partial(_pool_kernel, T=T, tc=tc),
        grid=(B,),
        in_specs=[
            pl.BlockSpec((None, T, POOL_WIDTH), lambda b: (b, 0, 0)),
            _layer(w_bd, l),
            _layer(scale, l),
        ],
        out_specs=pl.BlockSpec((None, T, POOL_WIDTH), lambda b: (b, 0, 0)),
        out_shape=jax.ShapeDtypeStruct((B, T, POOL_WIDTH), BF16),
        scratch_shapes=[pltpu.VMEM((T + 2 * POOL_HALO, POOL_WIDTH), F32)],
        compiler_params=_params("arbitrary"),
        name="pool",
    )(u, w_bd, scale)


def _fourier_factors(T):
    n1 = 1 << (int(math.log2(T)) // 2)
    return n1, T // n1


def _split_const(a):
    a = jnp.asarray(a, F32)
    hi = a.astype(BF16)
    return hi, (a - hi.astype(F32)).astype(BF16)


def _fourier_tables(T):
    n1, n2 = _fourier_factors(T)
    gd = FOURIER_GROUP_DIM
    t2 = np.arange(n2)[:, None, None]
    k1 = np.arange(n1)[None, :, None]
    t1 = np.arange(n1)[None, None, :]
    ph = -2.0 * np.pi * (t2 * k1 / T + t1 * k1 / n1)
    m1 = np.concatenate([np.cos(ph), np.sin(ph)], axis=1) / math.sqrt(n1)
    a3 = 2.0 * np.pi * np.outer(np.arange(n2), np.arange(n2)) / n2
    c3, s3 = np.cos(a3) / math.sqrt(n2), np.sin(a3) / math.sqrt(n2)
    m3 = np.block([[c3, s3], [-s3, c3]])
    c = np.arange(gd)
    ang = 2.0 * np.pi * np.outer(c, c) / gd
    eye = np.eye(FOURIER_WIDTH // gd)
    cd = np.concatenate([np.kron(eye, np.cos(ang)), np.kron(eye, np.sin(ang))], axis=0) / math.sqrt(gd)
    return _split_const(m1) + _split_const(m3) + _split_const(cd)


def _split(a):
    hi = a.astype(BF16)
    return hi, (a - hi.astype(F32)).astype(BF16)


def _dot3(a, b):
    return _dot(a[0], b[0]) + (_dot(a[1], b[0]) + _dot(a[0], b[1]))


def _fourier_kernel(f_ref, m1h_ref, m1l_ref, m3h_ref, m3l_ref, cdh_ref, cdl_ref, o_ref, a_ref, g_ref,
                    *, n1, n2, tc, unroll):
    T = n1 * n2
    wt = FOURIER_WIDTH // LANES
    lanes = lambda j: slice(j * LANES, (j + 1) * LANES)
    pitch = n1 + FOURIER_ROW_PAD

    def over_t1(t2, carry):
        rows = pl.ds(t2, n1, stride=n2)
        x = jnp.concatenate([f_ref[j, rows, :] for j in range(wt)], axis=1)
        a = _dot3((m1h_ref[t2], m1l_ref[t2]), _split(x))
        off = pl.multiple_of(t2 * pitch, SUBLANES)
        for j in range(wt):
            a_ref[j, pl.ds(off, n1), :] = a[:n1, lanes(j)]
            a_ref[wt + j, pl.ds(off, n1), :] = a[n1:, lanes(j)]
        return carry

    lax.fori_loop(0, n2, over_t1, 0, unroll=unroll)

    def over_t2(k1, carry):
        rows = pl.ds(k1, n2, stride=pitch)
        b = jnp.concatenate(
            [jnp.concatenate([a_ref[h * wt + j, rows, :] for j in range(wt)], axis=1) for h in range(2)], axis=0)
        g = _dot3((m3h_ref[...], m3l_ref[...]), _split(b))
        for j in range(wt):
            g_ref[j, rows, :] = g[:n2, lanes(j)]
            g_ref[wt + j, rows, :] = g[n2:, lanes(j)]
        return carry

    lax.fori_loop(0, n1, over_t2, 0, unroll=unroll)

    per = tc // n1
    for c in range(T // tc):
        groups = [slice((c * per + i) * pitch, (c * per + i) * pitch + n1) for i in range(per)]
        g = jnp.concatenate([jnp.concatenate([g_ref[j, r, :] for r in groups], axis=0) for j in range(2 * wt)],
                            axis=1)
        o_ref[c * tc:(c + 1) * tc, :] = _dot3(_split(g), (cdh_ref[...], cdl_ref[...])).astype(BF16)


def _fourier(f, tables):
    B, wt, T, _ = f.shape
    n1, n2 = _fourier_factors(T)
    return pl.pallas_call(
        functools.partial(_fourier_kernel, n1=n1, n2=n2, tc=min(TOKEN_TILE, T), unroll=FOURIER_UNROLL),
        grid=(B,),
        in_specs=[pl.BlockSpec((None, wt, T, LANES), lambda b: (b, 0, 0, 0))] + [_resident(t.shape) for t in tables],
        out_specs=pl.BlockSpec((None, T, FOURIER_WIDTH), lambda b: (b, 0, 0)),
        out_shape=jax.ShapeDtypeStruct((B, T, FOURIER_WIDTH), BF16),
        scratch_shapes=[pltpu.VMEM((2 * wt, n2 * (n1 + FOURIER_ROW_PAD), LANES), F32)] * 2,
        compiler_params=_params("arbitrary"),
        name="fourier",
    )(f, *tables)


def _merge_kernel(x_ref, o_ref, p_ref, f_ref, g_ref, sh_ref, sc_ref, gt_ref,
                  wg_ref, wa_ref, wp_ref, wf_ref, wo_ref, out_ref, y_ref):
    x = x_ref[...]
    h = _modnorm(x, g_ref[...], sc_ref[...], sh_ref[...]).astype(BF16)
    o = o_ref[...]
    p = p_ref[...]
    f = f_ref[...]
    for j in range(D_MODEL // MERGE_CHUNK):
        c = slice(j * MERGE_CHUNK, (j + 1) * MERGE_CHUNK)
        gcol = lambda b: IN_TOKEN_COLS + b * D_MODEL + j * MERGE_CHUNK
        gate = lambda b: jax.nn.sigmoid(_dot(h, wg_ref[:, gcol(b):gcol(b) + MERGE_CHUNK]))
        y = (gate(0) * _dot(o, wa_ref[:, c]) + gate(1) * _dot(p, wp_ref[:, c])
             + gate(2) * _dot(f, wf_ref[:, c]))
        y_ref[:, c] = y.astype(BF16)
    out_ref[...] = x + gt_ref[...] * _dot(y_ref[...], wo_ref[...])


def _merge(l, x, o, p, f, mod, gain, wg, wa, wp, wf, wo, ctx_row):
    B, T, _ = x.shape
    tm = min(MERGE_TILE, T)
    tok = lambda w: pl.BlockSpec((None, tm, w), lambda b, i: (b, i, 0))
    return pl.pallas_call(
        _merge_kernel,
        grid=(B, T // tm),
        in_specs=[
            tok(D_MODEL), tok(ATTN_WIDTH), tok(POOL_WIDTH), tok(FOURIER_WIDTH),
            _layer(gain, l),
            _mod_spec(l, 0, ctx_row), _mod_spec(l, 1, ctx_row), _mod_spec(l, 2, ctx_row),
            _layer(wg, l), _layer(wa, l), _layer(wp, l), _layer(wf, l), _layer(wo, l),
        ],
        out_specs=tok(D_MODEL),
        out_shape=jax.ShapeDtypeStruct((B, T, D_MODEL), F32),
        scratch_shapes=[pltpu.VMEM((tm, D_MODEL), BF16)],
        compiler_params=_params("arbitrary", "arbitrary"),
        name="merge",
    )(x, o, p, f, gain, mod, mod, mod, wg, wa, wp, wf, wo)


def _ffn_kernel(x_ref, xp_ref, xn_ref, g_ref, sh_ref, sc_ref, gt_ref, wu_ref, cw_ref, wd_ref, gf_ref,
                out_ref, h_ref, a_ref, *, tm, seg, final):
    i = pl.program_id(1)
    last = pl.num_programs(1) - 1
    g, sh, sc = g_ref[...], sh_ref[...], sc_ref[...]
    x = x_ref[...]
    H = SUBLANES
    h_ref[H:H + tm, :] = _modnorm(x, g, sc, sh).astype(BF16)
    keep_prev = (i > 0).astype(F32)
    keep_next = (i < last).astype(F32)
    h_ref[0:H, :] = (_modnorm(xp_ref[...], g, sc, sh) * keep_prev).astype(BF16)
    h_ref[H + tm:H + tm + H, :] = (_modnorm(xn_ref[...], g, sc, sh) * keep_next).astype(BF16)
    hf = h_ref[...]
    rows = tm + 2 * H

    if seg is not None:
        pos = (lax.broadcasted_iota(jnp.int32, (rows, 1), 0) - H) % seg
        tap_prev = (pos != 0).astype(F32)
        tap_next = (pos != seg - 1).astype(F32)

    def conv(up, cols):
        w = cw_ref[:, cols]
        before, after = pltpu.roll(up, 1, 0), pltpu.roll(up, rows - 1, 0)
        if seg is not None:
            before, after = before * tap_prev, after * tap_next
        y = before * w[0:1] + up * w[1:2] + after * w[2:3]
        return y[H:H + tm]

    for j in range(D_FF // FF_CHUNK):
        cv = slice(j * FF_CHUNK, (j + 1) * FF_CHUNK)
        cg = slice(D_FF + j * FF_CHUNK, D_FF + (j + 1) * FF_CHUNK)
        val = conv(_dot(hf, wu_ref[:, cv]), cv)
        gate = conv(_dot(hf, wu_ref[:, cg]), cg)
        a_ref[:, cv] = (val * (gate * jax.nn.sigmoid(gate))).astype(BF16)
    y = x + gt_ref[...] * _dot(a_ref[...], wd_ref[...])
    if final:
        ms = jnp.mean(y * y, axis=-1, keepdims=True)
        y = y * lax.rsqrt(ms + EPS) * gf_ref[...]
    out_ref[...] = y


def _ffn(l, x, mod, gain, wu, cw, wd, gain_final, ctx_row, final, seg=None):
    B, T, _ = x.shape
    tm = min(FFN_TILE, T)
    assert seg is None or (tm == T and T % seg == 0)
    per = tm // SUBLANES
    nblk = T // SUBLANES
    tok = pl.BlockSpec((None, tm, D_MODEL), lambda b, i: (b, i, 0))
    prev = pl.BlockSpec((None, SUBLANES, D_MODEL), lambda b, i: (b, jnp.maximum(i * per - 1, 0), 0))
    nxt = pl.BlockSpec((None, SUBLANES, D_MODEL), lambda b, i: (b, jnp.minimum((i + 1) * per, nblk - 1), 0))
    return pl.pallas_call(
        functools.partial(_ffn_kernel, tm=tm, seg=seg, final=final),
        grid=(B, T // tm),
        in_specs=[
            tok, prev, nxt,
            _layer(gain, l),
            _mod_spec(l, 3, ctx_row), _mod_spec(l, 4, ctx_row), _mod_spec(l, 5, ctx_row),
            _layer(wu, l), _layer(cw, l), _layer(wd, l),
            _resident((1, D_MODEL)),
        ],
        out_specs=tok,
        out_shape=jax.ShapeDtypeStruct((B, T, D_MODEL), F32),
        scratch_shapes=[pltpu.VMEM((tm + 2 * SUBLANES, D_MODEL), BF16), pltpu.VMEM((tm, D_FF), BF16)],
        compiler_params=_params("arbitrary", "arbitrary"),
        name="ffn_final" if final else "ffn",
    )(x, x, x, gain, mod, mod, mod, wu, cw, wd, gain_final)


def _rope_tables(T):
    rows = T // GRID_W
    row = np.repeat(np.arange(rows), GRID_W).astype(np.float64)
    col = np.tile(np.arange(GRID_W), rows).astype(np.float64)
    n_freq = HEAD_DIM // 4
    inv_freq = ROPE_BASE ** (-np.arange(n_freq) / n_freq)
    ar, ac = row[:, None] * inv_freq[None, :], col[:, None] * inv_freq[None, :]
    cos = np.concatenate([np.cos(ar), np.cos(ar), np.cos(ac), np.cos(ac)], axis=1)
    sin = np.concatenate([-np.sin(ar), np.sin(ar), -np.sin(ac), np.sin(ac)], axis=1)
    reps = LANES // HEAD_DIM
    return jnp.asarray(np.tile(cos, (1, reps)), F32), jnp.asarray(np.tile(sin, (1, reps)), F32)


def kernel(x, c, ctx, c_ctx, w_mod, b_mod, norm_mix, norm_ffn, w_in, attn_sink, pool_w, pool_scale,
           w_br_attn, w_br_pool, w_br_four, w_out, w_up, conv_w, w_down, norm_final):
    B, T, _ = x.shape
    L = ctx.shape[1]
    depth = w_mod.shape[0]
    assert B < COND_ROWS and T % max(INPROJ_TILE, MERGE_TILE, FFN_TILE) == 0 and T % (ATTN_Q_BLOCKS * BLOCK) == 0
    assert L % BLOCK == 0

    cond = jnp.concatenate([c, c_ctx[None], jnp.zeros((COND_ROWS - B - 1, D_MODEL), F32)], axis=0)
    mod = _adaln(cond, w_mod, b_mod).reshape(depth, COND_ROWS, N_MOD, 1, D_MODEL)

    cos_t, sin_t = _rope_tables(T)
    cos_c = jnp.ones((B * L, LANES), F32)
    sin_c = jnp.zeros((B * L, LANES), F32)
    four_lat = _fourier_tables(T)
    four_ctx = _fourier_tables(L)

    w_a = w_in.astype(BF16)
    w_a_br = (w_br_attn.reshape(depth, N_KV_HEADS, Q_PER_KV, HEAD_DIM, D_MODEL)
              .transpose(0, 2, 1, 3, 4).reshape(depth, ATTN_WIDTH, D_MODEL).astype(BF16))
    w_p_br = w_br_pool.astype(BF16)
    w_f_br = w_br_four.astype(BF16)
    w_o = w_out.astype(BF16)
    groups = pool_w.shape[1]
    w_pool = jnp.einsum("lgab,gh->lgahb", pool_w, jnp.eye(groups, dtype=pool_w.dtype)).reshape(
        depth, POOL_WIDTH, POOL_WIDTH).astype(BF16)
    p_scale = pool_scale.reshape(depth, 1, POOL_WIDTH)
    w_u = w_up.astype(BF16)
    w_d = w_down.astype(BF16)
    g_mix = norm_mix.reshape(depth, 1, D_MODEL)
    g_ffn = norm_ffn.reshape(depth, 1, D_MODEL)
    gain_final = norm_final.reshape(1, D_MODEL)

    xc = ctx
    for l in range(depth):
        last = l == depth - 1
        k, v, q, u, f = _inproj(l, x, mod, g_mix, w_a, cos_t, sin_t, None)
        flat = lambda a: a.reshape((1, B * L) + a.shape[2:])
        kc, vc, qc, uc, fc = _inproj(l, flat(xc), mod, g_mix, w_a, cos_c, sin_c, B)
        kc, vc, qc, uc = (a.reshape((B, L) + a.shape[2:]) for a in (kc, vc, qc, uc))
        fc = fc.reshape(FOURIER_WIDTH // LANES, B, L, LANES).transpose(1, 0, 2, 3)
        o = _attention(l, q, k, v, kc, vc, attn_sink, True)
        x = _merge(l, x, o, _pool(l, u, w_pool, p_scale), _fourier(f, four_lat), mod, g_mix,
                   w_a, w_a_br, w_p_br, w_f_br, w_o, None)
        if not last:
            oc = _attention(l, qc, None, None, kc, vc, attn_sink, False)
            xc = _merge(l, flat(xc), flat(oc), flat(_pool(l, uc, w_pool, p_scale)), flat(_fourier(fc, four_ctx)),
                        mod, g_mix, w_a, w_a_br, w_p_br, w_f_br, w_o, B).reshape(B, L, D_MODEL)
        x = _ffn(l, x, mod, g_ffn, w_u, conv_w, w_d, gain_final, None, last)
        if not last:
            xc = _ffn(l, flat(xc), mod, g_ffn, w_u, conv_w, w_d, gain_final, B, False, seg=L).reshape(B, L, D_MODEL)
    return x
```
